```python
import jax
import jax.numpy as jnp
from jax import lax
import numpy as np

D_MODEL = 1024
BATCH = 4
SEQ = 4096
DEPTH = 4

GRID_W = 64
CTX_LEN = 256

MLA_HEADS = 8
MLA_NOPE = 64
MLA_ROPE = 32
MLA_V = 64
MLA_Q_RANK = 256
MLA_KV_RANK = 128
ROPE_BASE = 10000.0
Q_BLOCK = 128

CONV_CH = 256
CONV_K = 3

GLA_HEADS = 4
GLA_DK = 32
GLA_DV = 64
GLA_GATE_RANK = 16
GLA_TAU = 16.0
GLA_CHUNK = 64

N_EXPERTS = 16
N_GROUPS = 4
TOP_K = 2
GROUP_SCORE_TOPK = 2
D_EXPERT = 512

N_BRANCH = 3
ALPHA = (2 * DEPTH) ** 0.25
BETA = (8 * DEPTH) ** -0.25
NORM_EPS = 1e-6
F32 = jnp.float32

IN_SPLITS = (
    MLA_Q_RANK,
    MLA_KV_RANK,
    MLA_ROPE,
    CONV_CH,
    CONV_CH,
    CONV_CH,
    GLA_HEADS * GLA_DK,
    GLA_HEADS * GLA_DK,
    GLA_HEADS * GLA_DV,
    GLA_HEADS * GLA_DV,
    GLA_GATE_RANK,
    GLA_GATE_RANK,
    N_BRANCH * D_MODEL,
)
D_IN = sum(IN_SPLITS)

kernel_name = 'hybrid_mla_conv_gla_moe_dit'


def _split_in(z):
    idx = np.cumsum(IN_SPLITS)[:-1].tolist()
    return jnp.split(z, idx, axis=-1)


def _modulate(x, shift, scale):
    return x * (1 + scale) + shift


def _modulation(cond, w_mod, b_mod):
    return jnp.split(jax.nn.silu(cond) @ w_mod + b_mod, 6, axis=-1)


def _layer_norm(x, g, b):
    xf = x.astype(F32)
    mu = xf.mean(-1, keepdims=True)
    var = jnp.square(xf - mu).mean(-1, keepdims=True)
    return ((xf - mu) * lax.rsqrt(var + NORM_EPS) * g.astype(F32) + b.astype(F32)).astype(x.dtype)


def _rms_norm(x, g):
    xf = x.astype(F32)
    return (xf * lax.rsqrt(jnp.mean(xf * xf, -1, keepdims=True) + NORM_EPS) * g.astype(F32)).astype(x.dtype)


def _axial_rope_tables(n_tokens):
    rows = n_tokens // GRID_W
    row = jnp.repeat(jnp.arange(rows, dtype=F32), GRID_W)
    col = jnp.tile(jnp.arange(GRID_W, dtype=F32), rows)
    half = MLA_ROPE // 2
    inv_freq = 1.0 / (ROPE_BASE ** (jnp.arange(0, half, 2, dtype=F32) / half))
    ang_r = row[:, None] * inv_freq[None, :]
    ang_c = col[:, None] * inv_freq[None, :]
    ang = jnp.concatenate([ang_r, ang_r, ang_c, ang_c], axis=-1)
    return jnp.cos(ang), jnp.sin(ang)


def _rope2d(x, cos, sin):
    x1, x2, x3, x4 = jnp.split(x, 4, axis=-1)
    rot = jnp.concatenate([-x2, x1, -x4, x3], axis=-1)
    return x * cos[:, None, :].astype(x.dtype) + rot * sin[:, None, :].astype(x.dtype)


def _mla_qkv(cq_in, ckv_in, kpe, g_q, w_qb, g_kv, w_kvb, rope):
    bn, t, _ = cq_in.shape
    q = (_rms_norm(cq_in, g_q) @ w_qb).reshape(bn, t, MLA_HEADS, MLA_NOPE + MLA_ROPE)
    kv = (_rms_norm(ckv_in, g_kv) @ w_kvb).reshape(bn, t, MLA_HEADS, MLA_NOPE + MLA_V)
    q_nope, q_pe = jnp.split(q, [MLA_NOPE], axis=-1)
    k_nope, v = jnp.split(kv, [MLA_NOPE], axis=-1)
    k_pe = kpe[:, :, None, :]
    if rope is not None:
        cos, sin = rope
        q_pe = _rope2d(q_pe, cos, sin)
        k_pe = _rope2d(k_pe, cos, sin)
    q = jnp.concatenate([q_nope, q_pe], axis=-1)
    k = jnp.concatenate([k_nope, jnp.broadcast_to(k_pe, (bn, t, MLA_HEADS, MLA_ROPE)).astype(k_nope.dtype)], axis=-1)
    return q, k, v


def _softmax_attend(q, k, v):
    s = jnp.einsum('bqhd,bkhd->bhqk', q, k).astype(F32) * (q.shape[-1] ** -0.5)
    p = jax.nn.softmax(s, axis=-1).astype(v.dtype)
    return jnp.einsum('bhqk,bkhv->bqhv', p, v)


def _blocked_attend(q, k, v):
    bn, t, h, dh = q.shape
    nb = t // Q_BLOCK
    qb = jnp.moveaxis(q.reshape(bn, nb, Q_BLOCK, h, dh), 1, 0)
    ob = lax.map(lambda qi: _softmax_attend(qi, k, v), qb)
    return jnp.moveaxis(ob, 0, 1).reshape(bn, t, h, v.shape[-1])


def _short_conv(bg, cg, h, w, b):
    y = cg * h
    t = y.shape[1]
    pad = CONV_K // 2
    yp = jnp.pad(y, ((0, 0), (pad, pad), (0, 0)))
    conv = b
    for j in range(CONV_K):
        conv = conv + yp[:, j:j + t] * w[j]
    return bg * conv


def _gla_heads(a, dh):
    bn, t, _ = a.shape
    return a.reshape(bn, t, GLA_HEADS, dh).transpose(0, 2, 1, 3).astype(F32)


def _gla_log_gate(low, w, b):
    return jax.nn.log_sigmoid((low @ w + b).astype(F32)) / GLA_TAU


def _gla_prep(z, w_gate, b_gate):
    gq, gk, gv, gf, gb = z[6], z[7], z[8], z[10], z[11]
    q = _gla_heads(gq, GLA_DK) * (GLA_DK ** -0.5)
    k = _gla_heads(gk, GLA_DK)
    v = _gla_heads(gv, GLA_DV)
    lf = _gla_heads(_gla_log_gate(gf, w_gate[0], b_gate[0]), GLA_DK)
    lb = _gla_heads(_gla_log_gate(gb, w_gate[1], b_gate[1]), GLA_DK)
    return q, k, v, lf, lb


def _gla_chunk(q, k, v, logg, s0):
    bn, h, t, dk = q.shape
    dv = v.shape[-1]
    n = t // GLA_CHUNK
    q, k, v, logg = (a.reshape(bn, h, n, GLA_CHUNK, a.shape[-1]) for a in (q, k, v, logg))
    bcum = jnp.cumsum(logg, axis=3)
    b_last = bcum[:, :, :, -1:, :]
    qd = q * jnp.exp(bcum)
    kd = k * jnp.exp(-bcum)
    kend = k * jnp.exp(b_last - bcum)
    lower = jnp.tril(jnp.ones((GLA_CHUNK, GLA_CHUNK), dtype=bool))
    att = jnp.where(lower, jnp.einsum('bhnid,bhnjd->bhnij', qd, kd), 0.0)
    o_intra = jnp.einsum('bhnij,bhnjv->bhniv', att, v)
    kv_chunk = jnp.einsum('bhncd,bhncv->bhndv', kend, v)
    decay = jnp.exp(b_last[:, :, :, 0, :])

    def step(s, inp):
        dec, kvc = inp
        return s * dec[..., None] + kvc, s

    s_final, s_start = lax.scan(step, s0, (jnp.moveaxis(decay, 2, 0), jnp.moveaxis(kv_chunk, 2, 0)))
    o_inter = jnp.einsum('bhncd,nbhdv->bhncv', qd, s_start)
    return (o_intra + o_inter).reshape(bn, h, t, dv), s_final


def _gla_direction(ctx_in, lat_in, reverse):
    if reverse:
        ctx_in = tuple(jnp.flip(a, axis=2) for a in ctx_in)
        lat_in = tuple(jnp.flip(a, axis=2) for a in lat_in)
    bn = ctx_in[0].shape[0]
    s0 = jnp.zeros((bn, GLA_HEADS, GLA_DK, GLA_DV), F32)
    oc, sc = _gla_chunk(*ctx_in, s0)
    ol, _ = _gla_chunk(*lat_in, sc)
    if reverse:
        oc, ol = jnp.flip(oc, axis=2), jnp.flip(ol, axis=2)
    return oc, ol


def _gla_out(o, r, g_norm):
    bn, h, t, dv = o.shape
    o = _rms_norm(o.transpose(0, 2, 1, 3), g_norm.reshape(h, dv)).reshape(bn, t, h * dv)
    return o.astype(r.dtype) * jax.nn.silu(r)


def _finish_mixer(x, z, att, gla_raw, gate1, p):
    cb, cc, ch, gr, gates = z[3], z[4], z[5], z[9], z[12]
    conv = _short_conv(cb, cc, ch, p['conv_w'], p['conv_b'])
    gla = _gla_out(gla_raw, gr, p['gla_norm'])
    ga, gcv, gg = jnp.split(jax.nn.sigmoid(gates), N_BRANCH, axis=-1)
    merged = (ga * (att @ p['mla_w_o']) + gcv * (conv @ p['conv_w_o']) + gg * (gla @ p['gla_w_o']))
    m = merged @ p['w_out']
    return _layer_norm(ALPHA * x + gate1 * m, p['ln1_g'], p['ln1_b'])


def _route(u, router_w, router_b):
    s = jax.nn.sigmoid((u @ router_w).astype(F32))
    sel = s + router_b.astype(F32)
    per_group = N_EXPERTS // N_GROUPS
    grp = sel.reshape(sel.shape[:-1] + (N_GROUPS, per_group))
    grp_score = lax.top_k(grp, GROUP_SCORE_TOPK)[0].sum(-1)
    best = jnp.argmax(grp_score, axis=-1)
    in_grp = (jnp.arange(N_EXPERTS) // per_group) == best[..., None]
    _, idx = lax.top_k(jnp.where(in_grp, sel, -jnp.inf), TOP_K)
    w = jnp.take_along_axis(s, idx, axis=-1)
    w = w / w.sum(-1, keepdims=True)
    combine = (jax.nn.one_hot(idx, N_EXPERTS, dtype=F32) * w[..., None]).sum(-2)
    return combine.astype(u.dtype)


def _moe(u, combine, wg, wu, wd):
    y = jnp.zeros_like(u)
    for e in range(N_EXPERTS):
        h = jax.nn.silu(u @ wg[e]) * (u @ wu[e])
        y = y + combine[..., e:e + 1] * (h @ wd[e])
    return y


def _ffn_sublayer(x, shift, scale, gate, p, router_w, router_b):
    u = _modulate(x, shift, scale)
    f = _moe(u, _route(u, router_w, router_b), p['exp_wg'], p['exp_wu'], p['exp_wd'])
    return _layer_norm(ALPHA * x + gate * f, p['ln2_g'], p['ln2_b'])


def _hybrid_layer(xc, xl, mods_c, mods_l, p, router_w, router_b, rope, ctx_out):
    sh1c, sc1c, g1c, sh2c, sc2c, g2c = mods_c
    sh1l, sc1l, g1l, sh2l, sc2l, g2l = mods_l
    zc = _split_in(_modulate(xc, sh1c, sc1c) @ p['w_in'])
    zl = _split_in(_modulate(xl, sh1l, sc1l) @ p['w_in'])
    bn, s_len = xl.shape[0], xl.shape[1]

    mla_w = (p['mla_q_norm'], p['mla_wq_b'], p['mla_kv_norm'], p['mla_wkv_b'])
    qc, kc, vc = _mla_qkv(zc[0], zc[1], zc[2], *mla_w, None)
    ql, kl, vl = _mla_qkv(zl[0], zl[1], zl[2], *mla_w, rope)
    att_l = _blocked_attend(ql, jnp.concatenate([kc, kl], axis=1),
                            jnp.concatenate([vc, vl], axis=1)).reshape(bn, s_len, MLA_HEADS * MLA_V)

    qgc, kgc, vgc, lfc, lbc = _gla_prep(zc, p['gla_w_gate'], p['gla_b_gate'])
    qgl, kgl, vgl, lfl, lbl = _gla_prep(zl, p['gla_w_gate'], p['gla_b_gate'])
    oc_f, ol_f = _gla_direction((qgc, kgc, vgc, lfc), (qgl, kgl, vgl, lfl), False)
    oc_b, ol_b = _gla_direction((qgc, kgc, vgc, lbc), (qgl, kgl, vgl, lbl), True)

    xl = _finish_mixer(xl, zl, att_l, ol_f + ol_b, g1l, p)
    xl = _ffn_sublayer(xl, sh2l, sc2l, g2l, p, router_w, router_b)

    if ctx_out:
        att_c = _softmax_attend(qc, kc, vc).reshape(bn, xc.shape[1], MLA_HEADS * MLA_V)
        xc = _finish_mixer(xc, zc, att_c, oc_f + oc_b, g1c, p)
        xc = _ffn_sublayer(xc, sh2c, sc2c, g2c, p, router_w, router_b)
    return xc, xl


def setup_inputs(seed: int = 0) -> dict:
    key = jax.random.key(seed)
    ks = iter(jax.random.split(key, 32))

    def nrm(shape, scale):
        return jax.random.normal(next(ks), shape, F32) * scale

    def gain(shape):
        return 1.0 + nrm(shape, 0.02)

    L, D = DEPTH, D_MODEL
    return {
        'x': nrm((BATCH, SEQ, D), 1.0),
        'c': nrm((BATCH, D), 1.0),
        'ctx': nrm((BATCH, CTX_LEN, D), 1.0),
        'c_ctx': nrm((D,), 1.0),
        'w_mod': nrm((L, D, 6 * D), 0.5 * D ** -0.5),
        'b_mod': nrm((L, 6 * D), 0.02),
        'w_in': nrm((L, D, D_IN), D ** -0.5),
        'mla_q_norm': gain((L, MLA_Q_RANK)),
        'mla_wq_b': nrm((L, MLA_Q_RANK, MLA_HEADS * (MLA_NOPE + MLA_ROPE)), MLA_Q_RANK ** -0.5),
        'mla_kv_norm': gain((L, MLA_KV_RANK)),
        'mla_wkv_b': nrm((L, MLA_KV_RANK, MLA_HEADS * (MLA_NOPE + MLA_V)), MLA_KV_RANK ** -0.5),
        'mla_w_o': nrm((L, MLA_HEADS * MLA_V, D), (MLA_HEADS * MLA_V) ** -0.5),
        'conv_w': nrm((L, CONV_K, CONV_CH), CONV_K ** -0.5),
        'conv_b': nrm((L, CONV_CH), 0.02),
        'conv_w_o': nrm((L, CONV_CH, D), CONV_CH ** -0.5),
        'gla_w_gate': nrm((L, 2, GLA_GATE_RANK, GLA_HEADS * GLA_DK), GLA_GATE_RANK ** -0.5),
        'gla_b_gate': nrm((L, 2, GLA_HEADS * GLA_DK), 0.02),
        'gla_norm': gain((L, GLA_HEADS * GLA_DV)),
        'gla_w_o': nrm((L, GLA_HEADS * GLA_DV, D), (GLA_HEADS * GLA_DV) ** -0.5),
        'w_out': nrm((L, D, D), BETA * D ** -0.5),
        'ln1_g': gain((L, D)),
        'ln1_b': nrm((L, D), 0.02),
        'router_w': nrm((D, N_EXPERTS), D ** -0.5),
        'router_b': nrm((N_EXPERTS,), 0.01),
        'exp_wg': nrm((L, N_EXPERTS, D, D_EXPERT), D ** -0.5),
        'exp_wu': nrm((L, N_EXPERTS, D, D_EXPERT), D ** -0.5),
        'exp_wd': nrm((L, N_EXPERTS, D_EXPERT, D), BETA * D_EXPERT ** -0.5),
        'ln2_g': gain((L, D)),
        'ln2_b': nrm((L, D), 0.02),
    }


def reference(x, c, ctx, c_ctx, w_mod, b_mod, w_in, mla_q_norm, mla_wq_b, mla_kv_norm, mla_wkv_b,
              mla_w_o, conv_w, conv_b, conv_w_o, gla_w_gate, gla_b_gate, gla_norm, gla_w_o, w_out,
              ln1_g, ln1_b, router_w, router_b, exp_wg, exp_wu, exp_wd, ln2_g, ln2_b):
    rope = _axial_rope_tables(x.shape[1])
    xl, xc = x, ctx
    for l in range(DEPTH):
        p = {
            'w_in': w_in[l], 'mla_q_norm': mla_q_norm[l], 'mla_wq_b': mla_wq_b[l],
            'mla_kv_norm': mla_kv_norm[l], 'mla_wkv_b': mla_wkv_b[l], 'mla_w_o': mla_w_o[l],
            'conv_w': conv_w[l], 'conv_b': conv_b[l], 'conv_w_o': conv_w_o[l],
            'gla_w_gate': gla_w_gate[l], 'gla_b_gate': gla_b_gate[l], 'gla_norm': gla_norm[l],
            'gla_w_o': gla_w_o[l], 'w_out': w_out[l], 'ln1_g': ln1_g[l], 'ln1_b': ln1_b[l],
            'exp_wg': exp_wg[l], 'exp_wu': exp_wu[l], 'exp_wd': exp_wd[l],
            'ln2_g': ln2_g[l], 'ln2_b': ln2_b[l],
        }
        mods_l = [m[:, None, :] for m in _modulation(c, w_mod[l], b_mod[l])]
        mods_c = _modulation(c_ctx, w_mod[l], b_mod[l])
        xc, xl = _hybrid_layer(xc, xl, mods_c, mods_l, p, router_w, router_b, rope, l < DEPTH - 1)
    return xl
```

```python
import functools

import numpy as np
import jax
import jax.numpy as jnp
from jax import lax
from jax.experimental import pallas as pl
from jax.experimental.pallas import tpu as pltpu

D_MODEL = 1024
GRID_W = 64
MLA_HEADS = 8
MLA_NOPE = 64
MLA_ROPE = 32
MLA_V = 64
MLA_Q_RANK = 256
MLA_KV_RANK = 128
ROPE_BASE = 10000.0
CONV_CH = 256
GLA_HEADS = 4
GLA_DK = 32
GLA_DV = 64
GLA_GATE_RANK = 16
GLA_TAU = 16.0
GLA_CHUNK = 64
N_EXPERTS = 16
N_GROUPS = 4
PER_GROUP = N_EXPERTS // N_GROUPS
D_EXPERT = 512
NORM_EPS = 1e-6
F32 = jnp.float32
BF16 = jnp.bfloat16

LANES = 128
TM = 256
HEAD_PAD = 128
QK_DIM = MLA_NOPE + MLA_ROPE
GLA_K = GLA_HEADS * GLA_DK
GLA_V = GLA_HEADS * GLA_DV
N_PAIRS = PER_GROUP * (PER_GROUP - 1) // 2
N_BUCKETS = N_GROUPS * N_PAIRS
BUCKET_ROWS = 32
EXT = D_MODEL + LANES
VMEM_LIMIT = 56 * 1024 * 1024

C_CQ = 0
C_CKV = C_CQ + MLA_Q_RANK
C_KPE = C_CKV + MLA_KV_RANK
C_CONV = C_KPE + 2 * LANES
C_GLA = C_CONV + 3 * CONV_CH
C_GATES = C_GLA + 2 * GLA_K + 2 * GLA_V + LANES
W_A_COLS = C_GATES + 3 * D_MODEL


def _params(*sem):
    return pltpu.CompilerParams(dimension_semantics=sem, vmem_limit_bytes=VMEM_LIMIT)


def _const_spec(shape):
    n = len(shape)
    return pl.BlockSpec(shape, lambda *_: (0,) * n, pipeline_mode=pl.Buffered(1))


def _dot(a, b):
    return jnp.dot(a, b, preferred_element_type=F32)


def _dot_nt(a, b):
    return lax.dot_general(a, b, (((1,), (1,)), ((), ())), preferred_element_type=F32)


def _dot_tn(a, b):
    return lax.dot_general(a, b, (((0,), (0,)), ((), ())), preferred_element_type=F32)


def _split_hi_lo(x):
    hi = x.astype(BF16)
    lo = (x - hi.astype(F32)).astype(BF16)
    return hi, lo


def _sigmoid(x):
    return 1.0 / (1.0 + jnp.exp(-x))


def _silu(x):
    return x * _sigmoid(x)


def _layer_norm(v, g, b):
    mu = jnp.mean(v, axis=-1, keepdims=True)
    d = v - mu
    var = jnp.mean(d * d, axis=-1, keepdims=True)
    return d * lax.rsqrt(var + NORM_EPS) * g + b


def _rms(v, g):
    return v * lax.rsqrt(jnp.mean(v * v, axis=-1, keepdims=True) + NORM_EPS) * g


def _mod_kernel(c_ref, w_ref, b_ref, o_ref):
    o_ref[...] = _dot(_silu(c_ref[...]).astype(BF16), w_ref[...].astype(BF16)) + b_ref[...]


def _modulation(cond, w_mod, b_mod):
    depth, d, n = w_mod.shape
    tn = n // 4
    return pl.pallas_call(
        _mod_kernel,
        grid=(depth, n // tn),
        in_specs=[pl.BlockSpec((8, d), lambda l, j: (0, 0)),
                  pl.BlockSpec((None, d, tn), lambda l, j: (l, 0, j)),
                  pl.BlockSpec((None, 1, tn), lambda l, j: (l, 0, j))],
        out_specs=pl.BlockSpec((None, 8, tn), lambda l, j: (l, 0, j)),
        out_shape=jax.ShapeDtypeStruct((depth, 8, n), F32),
        compiler_params=_params("parallel", "parallel"),
    )(cond, w_mod, b_mod.reshape(depth, 1, n))


def _in_proj_kernel(x_ref, mod_ref, rope_ref, wa_ref, gq_ref, wq_ref, gkv_ref, wkv_ref, wgate_ref, bgate_ref,
                    q_ref, k_ref, v_ref, y_ref, cb_ref, gl_ref, dec_ref, sr_ref, sg_ref):
    u = (x_ref[...] * (1.0 + mod_ref[1:2, :]) + mod_ref[0:1, :]).astype(BF16)
    cq_tab, sq_tab = rope_ref[:, 0:LANES], rope_ref[:, LANES:2 * LANES]
    ck_tab, sk_tab = rope_ref[:, 2 * LANES:3 * LANES], rope_ref[:, 3 * LANES:4 * LANES]

    cq = _dot(u, wa_ref[:, C_CQ:C_CKV])
    q2 = _dot(_rms(cq, gq_ref[...]).astype(BF16), wq_ref[...])
    nq = MLA_HEADS * HEAD_PAD
    for h in range(MLA_HEADS):
        a, b = h * HEAD_PAD, (h + 1) * HEAD_PAD
        q_ref[:, a:b] = (q2[:, a:b] * cq_tab + q2[:, nq + a:nq + b] * sq_tab).astype(BF16)

    ckv = _dot(u, wa_ref[:, C_CKV:C_KPE])
    kv = _dot(_rms(ckv, gkv_ref[...]).astype(BF16), wkv_ref[...])
    kpe2 = _dot(u, wa_ref[:, C_KPE:C_CONV])
    kpe = kpe2[:, 0:LANES] * ck_tab + kpe2[:, LANES:2 * LANES] * sk_tab
    for h in range(MLA_HEADS):
        a, b = h * HEAD_PAD, (h + 1) * HEAD_PAD
        k_ref[:, a:b] = (kv[:, a:b] + kpe).astype(BF16)
    v_ref[...] = kv[:, nq:nq + MLA_HEADS * MLA_V].astype(BF16)

    cv = _dot(u, wa_ref[:, C_CONV:C_GLA])
    cb_ref[...] = cv[:, 0:CONV_CH]
    y_ref[...] = cv[:, CONV_CH:2 * CONV_CH] * cv[:, 2 * CONV_CH:3 * CONV_CH]

    g = _dot(u, wa_ref[:, C_GLA:C_GATES])
    gq = g[:, 0:GLA_K] * (GLA_DK ** -0.5)
    gk = g[:, GLA_K:2 * GLA_K]
    gv = g[:, 2 * GLA_K:2 * GLA_K + GLA_V]
    gr = g[:, 2 * GLA_K + GLA_V:2 * GLA_K + 2 * GLA_V]
    low = g[:, 2 * GLA_K + 2 * GLA_V:]
    pre = _dot(low.astype(BF16), wgate_ref[...]) + bgate_ref[...]
    logg = (jnp.minimum(pre, 0.0) - jnp.log(1.0 + jnp.exp(-jnp.abs(pre)))) * (1.0 / GLA_TAU)
    lf, lb = logg[:, 0:GLA_K], logg[:, GLA_K:2 * GLA_K]
    tm = lf.shape[0]
    ri = lax.broadcasted_iota(jnp.int32, (tm, tm), 0)
    ci = lax.broadcasted_iota(jnp.int32, (tm, tm), 1)
    same = (ri // GLA_CHUNK) == (ci // GLA_CHUNK)
    m_low = jnp.where(same & (ci <= ri), 1.0, 0.0).astype(BF16)
    m_up = jnp.where(same & (ci >= ri), 1.0, 0.0).astype(BF16)
    pieces = jnp.concatenate(_split_hi_lo(lf) + _split_hi_lo(lb), axis=1)
    pm = _dot(m_low, pieces)
    pu = _dot(m_up, pieces)
    pre_f = pm[:, 0:GLA_K] + pm[:, GLA_K:2 * GLA_K]
    pre_b = pm[:, 2 * GLA_K:3 * GLA_K] + pm[:, 3 * GLA_K:4 * GLA_K]
    suf_f = pu[:, 0:GLA_K] + pu[:, GLA_K:2 * GLA_K]
    suf_b = pu[:, 2 * GLA_K:3 * GLA_K] + pu[:, 3 * GLA_K:4 * GLA_K]
    gl_ref[:, 0:GLA_K] = (gq * jnp.exp(pre_f)).astype(BF16)
    gl_ref[:, GLA_K:2 * GLA_K] = (gk * jnp.exp(-pre_f)).astype(BF16)
    gl_ref[:, 2 * GLA_K:3 * GLA_K] = (gk * jnp.exp(suf_f - lf)).astype(BF16)
    gl_ref[:, 3 * GLA_K:4 * GLA_K] = (gq * jnp.exp(suf_b)).astype(BF16)
    gl_ref[:, 4 * GLA_K:5 * GLA_K] = (gk * jnp.exp(-suf_b)).astype(BF16)
    gl_ref[:, 5 * GLA_K:6 * GLA_K] = (gk * jnp.exp(pre_b - lb)).astype(BF16)
    gl_ref[:, 6 * GLA_K:6 * GLA_K + GLA_V] = gv.astype(BF16)
    dec_ref[:, 0:GLA_K] = jnp.exp(pre_f + suf_f - lf)
    dec_ref[:, GLA_K:2 * GLA_K] = jnp.exp(pre_b + suf_b - lb)
    sr_ref[...] = _silu(gr)

    for j in range(3):
        a, b = j * D_MODEL, (j + 1) * D_MODEL
        sg_ref[:, a:b] = _sigmoid(_dot(u, wa_ref[:, C_GATES + a:C_GATES + b])).astype(BF16)


def _in_proj(x, mods, l, rope_tab, wa, gq, wq, gkv, wkv, wgate, bgate):
    bn, t, d = x.shape
    nt = t // TM
    tok = lambda w: pl.BlockSpec((None, TM, w), lambda b, i: (b, i, 0))
    outs = [(MLA_HEADS * HEAD_PAD, BF16), (MLA_HEADS * HEAD_PAD, BF16), (MLA_HEADS * MLA_V, BF16),
            (CONV_CH, F32), (CONV_CH, F32), (6 * GLA_K + GLA_V, BF16), (2 * GLA_K, F32), (GLA_V, F32),
            (3 * D_MODEL, BF16)]
    return pl.pallas_call(
        _in_proj_kernel,
        grid=(bn, nt),
        in_specs=[tok(d),
                  pl.BlockSpec((None, None, 6, d), lambda b, i: (l, jnp.where(i == 0, bn, b), 0, 0)),
                  pl.BlockSpec((TM, 4 * LANES), lambda b, i: (i, 0)),
                  _const_spec(wa.shape), _const_spec(gq.shape), _const_spec(wq.shape),
                  _const_spec(gkv.shape), _const_spec(wkv.shape), _const_spec(wgate.shape),
                  _const_spec(bgate.shape)],
        out_specs=[tok(w) for w, _ in outs],
        out_shape=[jax.ShapeDtypeStruct((bn, t, w), dt) for w, dt in outs],
        compiler_params=_params("parallel", "parallel"),
    )(x, mods, rope_tab, wa, gq, wq, gkv, wkv, wgate, bgate)


def _attn_kernel(q_ref, k_ref, v_ref, o_ref, *, ctx_len):
    def attend(nk):
        outs = []
        for j in range(2):
            q = q_ref[:, j * HEAD_PAD:(j + 1) * HEAD_PAD]
            k = k_ref[0:nk, j * HEAD_PAD:(j + 1) * HEAD_PAD]
            s = _dot_nt(q, k)
            p = jnp.exp(s - jnp.max(s, axis=-1, keepdims=True))
            den = jnp.sum(p, axis=-1, keepdims=True)
            outs.append(_dot(p.astype(BF16), v_ref[0:nk, :]) / den)
        lane = lax.broadcasted_iota(jnp.int32, outs[0].shape, 1)
        o_ref[...] = jnp.where(lane < MLA_V, outs[0], outs[1]).astype(o_ref.dtype)

    is_ctx = pl.program_id(2) == 0
    pl.when(is_ctx)(lambda: attend(ctx_len))
    pl.when(jnp.logical_not(is_ctx))(lambda: attend(k_ref.shape[0]))


def _attention(q, k, v):
    bn, t, _ = q.shape
    hp = MLA_HEADS // 2
    return pl.pallas_call(
        functools.partial(_attn_kernel, ctx_len=TM),
        grid=(bn, hp, t // TM),
        in_specs=[pl.BlockSpec((None, TM, 2 * HEAD_PAD), lambda b, h, i: (b, i, h)),
                  pl.BlockSpec((None, t, 2 * HEAD_PAD), lambda b, h, i: (b, 0, h)),
                  pl.BlockSpec((None, t, 2 * MLA_V), lambda b, h, i: (b, 0, h))],
        out_specs=pl.BlockSpec((None, TM, 2 * MLA_V), lambda b, h, i: (b, i, h)),
        out_shape=jax.ShapeDtypeStruct((bn, t, MLA_HEADS * MLA_V), BF16),
        compiler_params=_params("parallel", "parallel", "arbitrary"),
    )(q, k, v)


def _gla_kernel(qf_ref, vf_ref, df_ref, qb_ref, vb_ref, db_ref, of_ref, ob_ref, sf_ref, sb_ref):
    @pl.when(pl.program_id(1) == 0)
    def _():
        sf_ref[...] = jnp.zeros_like(sf_ref)
        sb_ref[...] = jnp.zeros_like(sb_ref)

    c = GLA_CHUNK
    lane_k = lax.broadcasted_iota(jnp.int32, (c, GLA_K), 1) // GLA_DK
    lane_v = lax.broadcasted_iota(jnp.int32, (c, GLA_V), 1) // GLA_DV
    ai = lax.broadcasted_iota(jnp.int32, (c, GLA_HEADS * c), 0)
    aj = lax.broadcasted_iota(jnp.int32, (c, GLA_HEADS * c), 1) % c
    state_mask = (lax.broadcasted_iota(jnp.int32, (GLA_V, GLA_K), 0) // GLA_DV ==
                  lax.broadcasted_iota(jnp.int32, (GLA_V, GLA_K), 1) // GLA_DK)
    zero = jnp.zeros((), BF16)

    def chunk(g_ref, v_ref, d_ref, o_ref, s_ref, r, causal):
        rows = slice(r * c, (r + 1) * c)
        qd = g_ref[rows, 0:GLA_K]
        kd = g_ref[rows, GLA_K:2 * GLA_K]
        kend = g_ref[rows, 2 * GLA_K:3 * GLA_K]
        v = v_ref[rows, :]
        dec = d_ref[r * c:r * c + 1, :]
        kd_heads = jnp.concatenate([jnp.where(lane_k == h, kd, zero) for h in range(GLA_HEADS)], axis=0)
        v_heads = jnp.concatenate([jnp.where(lane_v == h, v, zero) for h in range(GLA_HEADS)], axis=0)
        att = jnp.where(causal, _dot_nt(qd, kd_heads), 0.0)
        st = s_ref[...]
        o_ref[rows, :] = _dot(att.astype(BF16), v_heads) + _dot_nt(qd, st.astype(BF16))
        s_ref[...] = st * dec + jnp.where(state_mask, _dot_tn(v, kend), 0.0)

    n_chunks = qf_ref.shape[0] // c
    for r in range(n_chunks):
        chunk(qf_ref, vf_ref, df_ref, of_ref, sf_ref, r, aj <= ai)
    for r in reversed(range(n_chunks)):
        chunk(qb_ref, vb_ref, db_ref, ob_ref, sb_ref, r, aj >= ai)


def _gla(gl, dec):
    bn, t, _ = gl.shape
    nt = t // TM
    fwd = lambda i: i
    bwd = lambda i: jnp.where(i == 0, 0, nt - i)
    spec = lambda w, order, col: pl.BlockSpec((None, TM, w), lambda b, i: (b, order(i), col))
    return pl.pallas_call(
        _gla_kernel,
        grid=(bn, nt),
        in_specs=[spec(3 * GLA_K, fwd, 0), spec(GLA_V, fwd, 3), spec(GLA_K, fwd, 0),
                  spec(3 * GLA_K, bwd, 1), spec(GLA_V, bwd, 3), spec(GLA_K, bwd, 1)],
        out_specs=[spec(GLA_V, fwd, 0), spec(GLA_V, bwd, 0)],
        out_shape=[jax.ShapeDtypeStruct((bn, t, GLA_V), F32)] * 2,
        scratch_shapes=[pltpu.VMEM((GLA_V, GLA_K), F32)] * 2,
        compiler_params=_params("parallel", "arbitrary"),
    )(gl, gl, dec, gl, gl, dec)


def _finish_kernel(x_ref, mod_ref, y_ref, yp_ref, yn_ref, cb_ref, sr_ref, sg_ref, att_ref, of_ref, ob_ref,
                   wmla_ref, cw_ref, cbias_ref, wconv_ref, gnorm_ref, wgla_ref, wout_ref, lng_ref, lnb_ref,
                   o_ref, *, alpha):
    i = pl.program_id(1)
    nt = pl.num_programs(1)
    tm = y_ref.shape[0]
    y = y_ref[...]
    has_prev = jnp.logical_and(i != 0, i != 1)
    has_next = jnp.logical_and(i != 0, i != nt - 1)
    prev_row = jnp.where(has_prev, yp_ref[7:8, :], 0.0)
    next_row = jnp.where(has_next, yn_ref[0:1, :], 0.0)
    row = lax.broadcasted_iota(jnp.int32, y.shape, 0)
    y_m1 = jnp.where(row == 0, prev_row, pltpu.roll(y, 1, axis=0))
    y_p1 = jnp.where(row == tm - 1, next_row, pltpu.roll(y, tm - 1, axis=0))
    conv = cbias_ref[...] + y_m1 * cw_ref[0:1, :] + y * cw_ref[1:2, :] + y_p1 * cw_ref[2:3, :]
    conv = cb_ref[...] * conv

    o = of_ref[...] + ob_ref[...]
    gi = lax.broadcasted_iota(jnp.int32, (GLA_V, GLA_V), 0) // GLA_DV
    gj = lax.broadcasted_iota(jnp.int32, (GLA_V, GLA_V), 1) // GLA_DV
    grp = jnp.where(gi == gj, 1.0 / GLA_DV, 0.0).astype(BF16)
    hi, lo = _split_hi_lo(o * o)
    ms = _dot(hi, grp) + _dot(lo, grp)
    gla = o * lax.rsqrt(ms + NORM_EPS) * gnorm_ref[...] * sr_ref[...]

    d = x_ref.shape[1]
    merged = (sg_ref[:, 0:d].astype(F32) * _dot(att_ref[...], wmla_ref[...])
              + sg_ref[:, d:2 * d].astype(F32) * _dot(conv.astype(BF16), wconv_ref[...])
              + sg_ref[:, 2 * d:3 * d].astype(F32) * _dot(gla.astype(BF16), wgla_ref[...]))
    m = _dot(merged.astype(BF16), wout_ref[...])
    o_ref[...] = _layer_norm(alpha * x_ref[...] + mod_ref[2:3, :] * m, lng_ref[...], lnb_ref[...])


def _finish(x, mods, l, y, cb, sr, sg, att, o_f, o_b, wmla, cw, cbias, wconv, gnorm, wgla, wout, lng, lnb, alpha):
    bn, t, d = x.shape
    nt = t // TM
    tok = lambda w: pl.BlockSpec((None, TM, w), lambda b, i: (b, i, 0))
    rows8 = TM // 8
    consts = [wmla, cw, cbias, wconv, gnorm, wgla, wout, lng, lnb]
    return pl.pallas_call(
        functools.partial(_finish_kernel, alpha=alpha),
        grid=(bn, nt),
        in_specs=[tok(d),
                  pl.BlockSpec((None, None, 6, d), lambda b, i: (l, jnp.where(i == 0, bn, b), 0, 0)),
                  tok(CONV_CH),
                  pl.BlockSpec((None, 8, CONV_CH), lambda b, i: (b, jnp.maximum(i * rows8 - 1, 0), 0)),
                  pl.BlockSpec((None, 8, CONV_CH), lambda b, i: (b, jnp.minimum((i + 1) * rows8, nt * rows8 - 1), 0)),
                  tok(CONV_CH), tok(GLA_V), tok(3 * d), tok(MLA_HEADS * MLA_V), tok(GLA_V), tok(GLA_V)]
                 + [_const_spec(a.shape) for a in consts],
        out_specs=tok(d),
        out_shape=jax.ShapeDtypeStruct((bn, t, d), F32),
        compiler_params=_params("parallel", "parallel"),
    )(x, mods, y, y, y, cb, sr, sg, att, o_f, o_b, *consts)


def _router_kernel(x_ref, mod_ref, rwh_ref, rwl_ref, rb_ref, u_ref, br_ref, cnt_ref, base_ref):
    first = jnp.logical_and(pl.program_id(0) == 0, pl.program_id(1) == 0)

    @pl.when(first)
    def _():
        base_ref[...] = jnp.zeros_like(base_ref)

    tm = x_ref.shape[0]
    u = x_ref[...] * (1.0 + mod_ref[4:5, :]) + mod_ref[3:4, :]
    u_ref[:, 0:D_MODEL] = u
    hi, lo = _split_hi_lo(u)
    logits = _dot(hi, rwh_ref[...]) + _dot(lo, rwh_ref[...]) + _dot(hi, rwl_ref[...])
    s = _sigmoid(logits.T[0:N_EXPERTS, :])
    sel = s + rb_ref[...]

    sel_e = [sel[e:e + 1, :] for e in range(N_EXPERTS)]
    chosen, gscore = [], []
    for g in range(N_GROUPS):
        members = list(range(g * PER_GROUP, (g + 1) * PER_GROUP))
        picked = []
        for a in members:
            beaten = jnp.zeros((1, tm), F32)
            for b in members:
                if b != a:
                    wins = (sel_e[b] >= sel_e[a]) if b < a else (sel_e[b] > sel_e[a])
                    beaten = beaten + jnp.where(wins, 1.0, 0.0)
            picked.append(beaten < 2.0)
        chosen.append(picked)
        score = jnp.zeros((1, tm), F32)
        for a, p in zip(members, picked):
            score = score + jnp.where(p, sel_e[a], 0.0)
        gscore.append(score)
    best = []
    for g in range(N_GROUPS):
        ok = jnp.ones((1, tm), jnp.bool_)
        for h in range(N_GROUPS):
            if h < g:
                ok = jnp.logical_and(ok, gscore[g] > gscore[h])
            elif h > g:
                ok = jnp.logical_and(ok, gscore[g] >= gscore[h])
        best.append(ok)

    zero = jnp.zeros((1, tm), F32)
    w_lo, w_hi = zero, zero
    bucket_rows = []
    for g in range(N_GROUPS):
        taken = [jnp.logical_and(best[g], p) for p in chosen[g]]
        seen = jnp.zeros((1, tm), jnp.bool_)
        for a in range(PER_GROUP):
            sa = s[g * PER_GROUP + a:g * PER_GROUP + a + 1, :]
            w_lo = w_lo + jnp.where(jnp.logical_and(taken[a], jnp.logical_not(seen)), sa, 0.0)
            w_hi = w_hi + jnp.where(jnp.logical_and(taken[a], seen), sa, 0.0)
            seen = jnp.logical_or(seen, taken[a])
        for a in range(PER_GROUP):
            for b in range(a + 1, PER_GROUP):
                bucket_rows.append(jnp.where(jnp.logical_and(taken[a], taken[b]), 1.0, 0.0))
    total = w_lo + w_hi
    w_lo, w_hi = w_lo / total, w_hi / total
    wts = jnp.concatenate([w_lo, w_hi, jnp.zeros((LANES - 2, tm), F32)], axis=0)
    u_ref[:, D_MODEL:D_MODEL + LANES] = wts.T

    onehot = jnp.concatenate(bucket_rows + [jnp.zeros((BUCKET_ROWS - N_BUCKETS, tm), F32)], axis=0)
    ri = lax.broadcasted_iota(jnp.int32, (tm, tm), 0)
    ci = lax.broadcasted_iota(jnp.int32, (tm, tm), 1)
    before = _dot(onehot.astype(BF16), jnp.where(ri < ci, 1.0, 0.0).astype(BF16))
    base = base_ref[:, 0:1]
    rank = jnp.sum(onehot * (before + base), axis=0, keepdims=True)
    bidx = lax.broadcasted_iota(jnp.int32, onehot.shape, 0).astype(F32)
    bucket = jnp.sum(onehot * bidx, axis=0, keepdims=True)
    br_ref[...] = jnp.concatenate([bucket, rank, jnp.zeros((6, tm), F32)], axis=0).astype(jnp.int32)
    new_base = base + jnp.sum(onehot, axis=1, keepdims=True)
    base_ref[...] = jnp.broadcast_to(new_base, base_ref.shape)
    cnt_ref[...] = jnp.broadcast_to(new_base, cnt_ref.shape).astype(jnp.int32)


def _router(x1, mods, l, rw_hi, rw_lo, rb):
    bn, t, d = x1.shape
    nt = t // TM
    return pl.pallas_call(
        _router_kernel,
        grid=(bn, nt),
        in_specs=[pl.BlockSpec((None, TM, d), lambda b, i: (b, i, 0)),
                  pl.BlockSpec((None, None, 6, d), lambda b, i: (l, jnp.where(i == 0, bn, b), 0, 0)),
                  _const_spec(rw_hi.shape), _const_spec(rw_lo.shape), _const_spec(rb.shape)],
        out_specs=[pl.BlockSpec((None, TM, EXT), lambda b, i: (b, i, 0)),
                   pl.BlockSpec((None, None, 8, TM), lambda b, i: (b, i, 0, 0)),
                   pl.BlockSpec((BUCKET_ROWS, LANES), lambda b, i: (0, 0))],
        out_shape=[jax.ShapeDtypeStruct((bn, t, EXT), F32),
                   jax.ShapeDtypeStruct((bn, nt, 8, TM), jnp.int32),
                   jax.ShapeDtypeStruct((BUCKET_ROWS, LANES), jnp.int32)],
        scratch_shapes=[pltpu.VMEM((BUCKET_ROWS, LANES), F32)],
        compiler_params=_params("arbitrary", "arbitrary"),
    )(x1, mods, rw_hi, rw_lo, rb)


def _place_kernel(off_ref, bucket_ref, rank_ref, pos_ref):
    bucket = bucket_ref[...]
    pos = rank_ref[...]
    for b in range(N_BUCKETS):
        pos = pos + jnp.where(bucket == b, off_ref[b], 0)
    pos_ref[...] = pos


def _place(offsets, bucket, rank):
    return pl.pallas_call(
        _place_kernel,
        in_specs=[pl.BlockSpec(memory_space=pltpu.SMEM), pl.BlockSpec(memory_space=pltpu.VMEM),
                  pl.BlockSpec(memory_space=pltpu.VMEM)],
        out_specs=pl.BlockSpec(memory_space=pltpu.VMEM),
        out_shape=jax.ShapeDtypeStruct(bucket.shape, jnp.int32),
    )(offsets, bucket, rank)


def _scatter_kernel(pos_ref, u_ref, init_ref, xs_ref, sem):
    del init_ref
    tm = u_ref.shape[0]

    def row_copy(r, dst_row):
        return pltpu.make_async_copy(u_ref.at[pl.ds(r, 1)], xs_ref.at[pl.ds(dst_row, 1)], sem)

    def issue(r, carry):
        row_copy(r, pos_ref[0, r]).start()
        return carry

    lax.fori_loop(0, tm, issue, 0)

    def drain(r, carry):
        row_copy(r, 0).wait()
        return carry

    lax.fori_loop(0, tm, drain, 0)


def _scatter(pos, u_ext, n_sorted):
    n_tiles = pos.shape[0]
    u2 = u_ext.reshape(n_tiles * TM, EXT)
    return pl.pallas_call(
        _scatter_kernel,
        grid=(n_tiles,),
        in_specs=[pl.BlockSpec((None, 1, TM), lambda i: (i, 0, 0), memory_space=pltpu.SMEM),
                  pl.BlockSpec((TM, EXT), lambda i: (i, 0)),
                  pl.BlockSpec(memory_space=pl.ANY)],
        out_specs=pl.BlockSpec(memory_space=pl.ANY),
        out_shape=jax.ShapeDtypeStruct((n_sorted, EXT), F32),
        scratch_shapes=[pltpu.SemaphoreType.DMA],
        input_output_aliases={2: 0},
        compiler_params=_params("arbitrary"),
    )(pos.reshape(n_tiles, 1, TM), u2, jnp.zeros((n_sorted, EXT), F32))


def _expert_kernel(e1_ref, e2_ref, used_ref, xs_ref, wg1_ref, wu1_ref, wd1_ref, wg2_ref, wu2_ref, wd2_ref, y_ref):
    del e1_ref, e2_ref
    live = pl.program_id(0) < used_ref[0]

    @pl.when(live)
    def _():
        x = xs_ref[:, 0:D_MODEL].astype(BF16)
        w_lo = xs_ref[:, D_MODEL:D_MODEL + 1]
        w_hi = xs_ref[:, D_MODEL + 1:D_MODEL + 2]
        h1 = _silu(_dot(x, wg1_ref[...])) * _dot(x, wu1_ref[...])
        h2 = _silu(_dot(x, wg2_ref[...])) * _dot(x, wu2_ref[...])
        y_ref[...] = w_lo * _dot(h1.astype(BF16), wd1_ref[...]) + w_hi * _dot(h2.astype(BF16), wd2_ref[...])

    @pl.when(jnp.logical_not(live))
    def _():
        y_ref[...] = jnp.zeros_like(y_ref)


def _experts(tile_e1, tile_e2, n_used, xs, wg, wu, wd, l):
    n_sorted = xs.shape[0]
    d, de = wg.shape[1], wg.shape[2]
    up = lambda tbl: pl.BlockSpec((None, d, de), lambda i, e1, e2, n: (l * N_EXPERTS + tbl(e1, e2)[i], 0, 0))
    down = lambda tbl: pl.BlockSpec((None, de, d), lambda i, e1, e2, n: (l * N_EXPERTS + tbl(e1, e2)[i], 0, 0))
    first = lambda e1, e2: e1
    second = lambda e1, e2: e2
    return pl.pallas_call(
        _expert_kernel,
        grid_spec=pltpu.PrefetchScalarGridSpec(
            num_scalar_prefetch=3,
            grid=(n_sorted // TM,),
            in_specs=[pl.BlockSpec((TM, EXT), lambda i, e1, e2, n: (i, 0)),
                      up(first), up(first), down(first), up(second), up(second), down(second)],
            out_specs=pl.BlockSpec((TM, d), lambda i, e1, e2, n: (i, 0)),
        ),
        out_shape=jax.ShapeDtypeStruct((n_sorted, d), F32),
        compiler_params=_params("arbitrary"),
    )(tile_e1, tile_e2, n_used, xs, wg, wu, wd, wg, wu, wd)


def _ffn_out_kernel(pos_ref, x_ref, mod_ref, ys_ref, lng_ref, lnb_ref, o_ref, buf_ref, sem, *, alpha):
    tm = x_ref.shape[0]

    def row_copy(r, src_row):
        return pltpu.make_async_copy(ys_ref.at[pl.ds(src_row, 1)], buf_ref.at[pl.ds(r, 1)], sem)

    def issue(r, carry):
        row_copy(r, pos_ref[0, r]).start()
        return carry

    lax.fori_loop(0, tm, issue, 0)

    def drain(r, carry):
        row_copy(r, 0).wait()
        return carry

    lax.fori_loop(0, tm, drain, 0)
    o_ref[...] = _layer_norm(alpha * x_ref[...] + mod_ref[5:6, :] * buf_ref[...], lng_ref[...], lnb_ref[...])


def _ffn_out(pos, x1, mods, l, ys, lng, lnb, alpha):
    bn, t, d = x1.shape
    nt = t // TM
    return pl.pallas_call(
        functools.partial(_ffn_out_kernel, alpha=alpha),
        grid=(bn, nt),
        in_specs=[pl.BlockSpec((None, 1, TM), lambda b, i: (b * nt + i, 0, 0), memory_space=pltpu.SMEM),
                  pl.BlockSpec((None, TM, d), lambda b, i: (b, i, 0)),
                  pl.BlockSpec((None, None, 6, d), lambda b, i: (l, jnp.where(i == 0, bn, b), 0, 0)),
                  pl.BlockSpec(memory_space=pl.ANY),
                  _const_spec(lng.shape), _const_spec(lnb.shape)],
        out_specs=pl.BlockSpec((None, TM, d), lambda b, i: (b, i, 0)),
        out_shape=jax.ShapeDtypeStruct((bn, t, d), F32),
        scratch_shapes=[pltpu.VMEM((TM, d), F32), pltpu.SemaphoreType.DMA],
        compiler_params=_params("arbitrary", "arbitrary"),
    )(pos.reshape(bn * nt, 1, TM), x1, mods, ys, lng, lnb)


def _rope_tables(ctx_len, seq_len):
    t = np.arange(seq_len)
    half = MLA_ROPE // 2
    inv_freq = 1.0 / (ROPE_BASE ** (np.arange(0, half, 2, dtype=np.float32) / half))
    ang_r = (t // GRID_W).astype(np.float32)[:, None] * inv_freq[None, :]
    ang_c = (t % GRID_W).astype(np.float32)[:, None] * inv_freq[None, :]
    ang = np.concatenate([ang_r, ang_r, ang_c, ang_c], axis=-1).astype(np.float32)
    cos = np.concatenate([np.ones((ctx_len, MLA_ROPE), np.float32), np.cos(ang)], axis=0)
    sin = np.concatenate([np.zeros((ctx_len, MLA_ROPE), np.float32), np.sin(ang)], axis=0)
    n = ctx_len + seq_len
    c_tab = np.zeros((n, LANES), np.float32)
    s_tab = np.zeros((n, LANES), np.float32)
    c_tab[:, :MLA_NOPE] = 1.0
    c_tab[:, MLA_NOPE:QK_DIM] = cos
    s_tab[:, MLA_NOPE:QK_DIM] = sin
    scale = np.float32(QK_DIM ** -0.5)
    return jnp.asarray(np.concatenate([c_tab * scale, s_tab * scale, c_tab, s_tab], axis=1))


def _rotate_cols(w):
    q = MLA_ROPE // 4
    w1, w2, w3, w4 = (w[..., i * q:(i + 1) * q] for i in range(4))
    return jnp.concatenate([-w2, w1, -w4, w3], axis=-1)


def _prep_layer_weights(w_in, mla_wq_b, mla_wkv_b, gla_w_gate, gla_b_gate):
    depth, d, _ = w_in.shape
    splits = np.cumsum([0, MLA_Q_RANK, MLA_KV_RANK, MLA_ROPE, CONV_CH, CONV_CH, CONV_CH, GLA_K, GLA_K, GLA_V,
                        GLA_V, GLA_GATE_RANK, GLA_GATE_RANK, 3 * D_MODEL])
    seg = [w_in[:, :, splits[i]:splits[i + 1]] for i in range(13)]
    zeros = lambda *s: jnp.zeros(s, F32)
    pad_rope = lambda w: jnp.concatenate([zeros(depth, d, MLA_NOPE), w, zeros(depth, d, LANES - QK_DIM)], axis=-1)
    wa = jnp.concatenate(
        [seg[0], seg[1], pad_rope(seg[2]), pad_rope(_rotate_cols(seg[2])), seg[3], seg[4], seg[5], seg[6], seg[7],
         seg[8], seg[9], seg[10], seg[11], zeros(depth, d, LANES - 2 * GLA_GATE_RANK), seg[12]], axis=-1).astype(BF16)

    r = mla_wq_b.shape[1]
    wq = mla_wq_b.reshape(depth, r, MLA_HEADS, QK_DIM)
    q_main = jnp.concatenate([wq, zeros(depth, r, MLA_HEADS, HEAD_PAD - QK_DIM)], axis=-1)
    q_rot = jnp.concatenate([zeros(depth, r, MLA_HEADS, MLA_NOPE), _rotate_cols(wq[..., MLA_NOPE:]),
                             zeros(depth, r, MLA_HEADS, HEAD_PAD - QK_DIM)], axis=-1)
    wq2 = jnp.concatenate([q_main.reshape(depth, r, -1), q_rot.reshape(depth, r, -1)], axis=-1).astype(BF16)

    rk = mla_wkv_b.shape[1]
    wkv = mla_wkv_b.reshape(depth, rk, MLA_HEADS, MLA_NOPE + MLA_V)
    k_part = jnp.concatenate([wkv[..., :MLA_NOPE], zeros(depth, rk, MLA_HEADS, HEAD_PAD - MLA_NOPE)], axis=-1)
    wkv2 = jnp.concatenate([k_part.reshape(depth, rk, -1), wkv[..., MLA_NOPE:].reshape(depth, rk, -1)],
                           axis=-1).astype(BF16)

    gr = GLA_GATE_RANK
    wgate = jnp.zeros((depth, LANES, 2 * GLA_K), F32)
    wgate = wgate.at[:, 0:gr, 0:GLA_K].set(gla_w_gate[:, 0]).at[:, gr:2 * gr, GLA_K:].set(gla_w_gate[:, 1])
    bgate = gla_b_gate.reshape(depth, 1, 2 * GLA_K)
    return wa, wq2, wkv2, wgate.astype(BF16), bgate


def _bucket_tables(counts, n_sorted_tiles):
    padded = ((counts + TM - 1) // TM) * TM
    ends = jnp.cumsum(padded)
    offsets = ends - padded
    n_used = (ends[-1] // TM).astype(jnp.int32)
    tile_start = jnp.arange(n_sorted_tiles, dtype=jnp.int32) * TM
    tile_bucket = jnp.sum((tile_start[:, None] >= ends[None, :]).astype(jnp.int32), axis=1)
    last_bucket = jnp.sum((jnp.maximum(ends[-1] - TM, 0) >= ends).astype(jnp.int32))
    tile_bucket = jnp.where(tile_start < ends[-1], tile_bucket, last_bucket)
    pair_lo = np.array([a for a in range(PER_GROUP) for b in range(a + 1, PER_GROUP)], np.int32)
    pair_hi = np.array([b for a in range(PER_GROUP) for b in range(a + 1, PER_GROUP)], np.int32)
    group = tile_bucket // N_PAIRS
    pair = tile_bucket % N_PAIRS
    e1 = group * PER_GROUP + jnp.asarray(pair_lo)[pair]
    e2 = group * PER_GROUP + jnp.asarray(pair_hi)[pair]
    return offsets.astype(jnp.int32), e1.astype(jnp.int32), e2.astype(jnp.int32), n_used.reshape(1)


def kernel(x, c, ctx, c_ctx, w_mod, b_mod, w_in, mla_q_norm, mla_wq_b, mla_kv_norm, mla_wkv_b, mla_w_o, conv_w,
           conv_b, conv_w_o, gla_w_gate, gla_b_gate, gla_norm, gla_w_o, w_out, ln1_g, ln1_b, router_w, router_b,
           exp_wg, exp_wu, exp_wd, ln2_g, ln2_b):
    bn, seq_len, d = x.shape
    ctx_len = ctx.shape[1]
    depth = w_in.shape[0]
    assert ctx_len == TM and seq_len % TM == 0 and d == D_MODEL and bn < 8
    alpha = (2 * depth) ** 0.25
    t = ctx_len + seq_len
    n_tok = bn * t
    n_sorted_tiles = n_tok // TM + N_BUCKETS
    n_sorted = n_sorted_tiles * TM

    cond = jnp.concatenate([c, c_ctx[None, :], jnp.zeros((7 - bn, d), F32)], axis=0)
    mods = _modulation(cond, w_mod, b_mod).reshape(depth, 8, 6, d)
    rope_tab = _rope_tables(ctx_len, seq_len)
    wa, wq2, wkv2, wgate, bgate = _prep_layer_weights(w_in, mla_wq_b, mla_wkv_b, gla_w_gate, gla_b_gate)
    row = lambda a: a.reshape(depth, 1, a.shape[-1])
    gq, gkv, cbias, gnorm = row(mla_q_norm), row(mla_kv_norm), row(conv_b), row(gla_norm)
    lng1, lnb1, lng2, lnb2 = row(ln1_g), row(ln1_b), row(ln2_g), row(ln2_b)
    wmla, wconv, wgla, wout = (a.astype(BF16) for a in (mla_w_o, conv_w_o, gla_w_o, w_out))
    rw = jnp.concatenate([router_w, jnp.zeros((d, LANES - N_EXPERTS), F32)], axis=1)
    rw_hi = rw.astype(BF16)
    rw_lo = (rw - rw_hi.astype(F32)).astype(BF16)
    rb = router_b.reshape(N_EXPERTS, 1)
    de = exp_wg.shape[-1]
    wg = exp_wg.astype(BF16).reshape(depth * N_EXPERTS, d, de)
    wu = exp_wu.astype(BF16).reshape(depth * N_EXPERTS, d, de)
    wd = exp_wd.astype(BF16).reshape(depth * N_EXPERTS, de, d)

    xt = jnp.concatenate([ctx, x], axis=1)
    for l in range(depth):
        q, k, v, y, cb, gl, dec, sr, sg = _in_proj(xt, mods, l, rope_tab, wa[l], gq[l], wq2[l], gkv[l], wkv2[l],
                                                   wgate[l], bgate[l])
        att = _attention(q, k, v)
        o_f, o_b = _gla(gl, dec)
        x1 = _finish(xt, mods, l, y, cb, sr, sg, att, o_f, o_b, wmla[l], conv_w[l], cbias[l], wconv[l], gnorm[l],
                     wgla[l], wout[l], lng1[l], lnb1[l], alpha)
        u_ext, br, counts = _router(x1, mods, l, rw_hi, rw_lo, rb)
        offsets, tile_e1, tile_e2, n_used = _bucket_tables(counts[:N_BUCKETS, 0], n_sorted_tiles)
        br = br.reshape(bn * (t // TM), 8, TM)
        pos = _place(offsets, br[:, 0, :], br[:, 1, :])
        xs = _scatter(pos, u_ext, n_sorted)
        ys = _experts(tile_e1, tile_e2, n_used, xs, wg, wu, wd, l)
        xt = _ffn_out(pos, x1, mods, l, ys, lng2[l], lnb2[l], alpha)
    return xt[:, ctx_len:, :]
```

```python
import functools

import numpy as np
import jax
import jax.numpy as jnp
from jax import lax
from jax.experimental import pallas as pl
from jax.experimental.pallas import tpu as pltpu

D_MODEL = 1024
GRID_W = 64
MLA_HEADS = 8
MLA_NOPE = 64
MLA_ROPE = 32
MLA_V = 64
MLA_Q_RANK = 256
MLA_KV_RANK = 128
ROPE_BASE = 10000.0
CONV_CH = 256
GLA_HEADS = 4
GLA_DK = 32
GLA_DV = 64
GLA_GATE_RANK = 16
GLA_TAU = 16.0
GLA_CHUNK = 64
N_EXPERTS = 16
N_GROUPS = 4
PER_GROUP = N_EXPERTS // N_GROUPS
D_EXPERT = 512
NORM_EPS = 1e-6
F32 = jnp.float32
BF16 = jnp.bfloat16

LANES = 128
TM = 256
HEAD_PAD = 128
QK_DIM = MLA_NOPE + MLA_ROPE
GLA_K = GLA_HEADS * GLA_DK
GLA_V = GLA_HEADS * GLA_DV
N_PAIRS = PER_GROUP * (PER_GROUP - 1) // 2
N_BUCKETS = N_GROUPS * N_PAIRS
BUCKET_ROWS = 32
TOKEN_ROWS = D_MODEL // LANES
XS_ROWS = 2 * TOKEN_ROWS
ISSUE_UNROLL = 8
ATTN_KEY_CHUNK = 256
VMEM_LIMIT = 56 * 1024 * 1024

C_CQ = 0
C_CKV = C_CQ + MLA_Q_RANK
C_KPE = C_CKV + MLA_KV_RANK
C_CONV = C_KPE + 2 * LANES
C_GLA = C_CONV + 3 * CONV_CH
C_GATES = C_GLA + 2 * GLA_K + 2 * GLA_V + LANES
W_A_COLS = C_GATES + 3 * D_MODEL


def _params(*sem):
    return pltpu.CompilerParams(dimension_semantics=sem, vmem_limit_bytes=VMEM_LIMIT)


def _const_spec(shape):
    n = len(shape)
    return pl.BlockSpec(shape, lambda *_: (0,) * n, pipeline_mode=pl.Buffered(1))


def _dot(a, b):
    return jnp.dot(a, b, preferred_element_type=F32)


def _dot_nt(a, b):
    return lax.dot_general(a, b, (((1,), (1,)), ((), ())), preferred_element_type=F32)


def _dot_tn(a, b):
    return lax.dot_general(a, b, (((0,), (0,)), ((), ())), preferred_element_type=F32)


def _split_hi_lo(x):
    hi = x.astype(BF16)
    lo = (x - hi.astype(F32)).astype(BF16)
    return hi, lo


def _sigmoid(x):
    return 1.0 / (1.0 + jnp.exp(-x))


def _silu(x):
    return x * _sigmoid(x)


def _layer_norm(v, g, b):
    mu = jnp.mean(v, axis=-1, keepdims=True)
    d = v - mu
    var = jnp.mean(d * d, axis=-1, keepdims=True)
    return d * lax.rsqrt(var + NORM_EPS) * g + b


def _rms(v, g):
    return v * lax.rsqrt(jnp.mean(v * v, axis=-1, keepdims=True) + NORM_EPS) * g


def _mod_kernel(c_ref, w_ref, b_ref, o_ref):
    o_ref[...] = _dot(_silu(c_ref[...]).astype(BF16), w_ref[...].astype(BF16)) + b_ref[...]


def _modulation(cond, w_mod, b_mod):
    depth, d, n = w_mod.shape
    tn = n // 4
    return pl.pallas_call(
        _mod_kernel,
        grid=(depth, n // tn),
        in_specs=[pl.BlockSpec((8, d), lambda l, j: (0, 0)),
                  pl.BlockSpec((None, d, tn), lambda l, j: (l, 0, j)),
                  pl.BlockSpec((None, 1, tn), lambda l, j: (l, 0, j))],
        out_specs=pl.BlockSpec((None, 8, tn), lambda l, j: (l, 0, j)),
        out_shape=jax.ShapeDtypeStruct((depth, 8, n), F32),
        compiler_params=_params("parallel", "parallel"),
        name="modulation",
    )(cond, w_mod, b_mod.reshape(depth, 1, n))


def _in_proj_kernel(x_ref, mod_ref, rope_ref, wa_ref, gq_ref, wq_ref, gkv_ref, wk_ref, wvt_ref, wgate_ref, bgate_ref,
                    q_ref, k_ref, vt_ref, y_ref, cb_ref, gl_ref, dec_ref, sr_ref, sg_ref):
    u = (x_ref[...] * (1.0 + mod_ref[1:2, :]) + mod_ref[0:1, :]).astype(BF16)
    cq_tab, sq_tab = rope_ref[:, 0:LANES], rope_ref[:, LANES:2 * LANES]
    ck_tab, sk_tab = rope_ref[:, 2 * LANES:3 * LANES], rope_ref[:, 3 * LANES:4 * LANES]

    cq = _dot(u, wa_ref[:, C_CQ:C_CKV])
    q2 = _dot(_rms(cq, gq_ref[...]).astype(BF16), wq_ref[...])
    nq = MLA_HEADS * HEAD_PAD
    for h in range(MLA_HEADS):
        a, b = h * HEAD_PAD, (h + 1) * HEAD_PAD
        q_ref[:, a:b] = (q2[:, a:b] * cq_tab + q2[:, nq + a:nq + b] * sq_tab).astype(BF16)

    ckv = _dot(u, wa_ref[:, C_CKV:C_KPE])
    ckvn = _rms(ckv, gkv_ref[...]).astype(BF16)
    kn = _dot(ckvn, wk_ref[...])
    kpe2 = _dot(u, wa_ref[:, C_KPE:C_CONV])
    kpe = kpe2[:, 0:LANES] * ck_tab + kpe2[:, LANES:2 * LANES] * sk_tab
    for h in range(MLA_HEADS):
        a, b = h * HEAD_PAD, (h + 1) * HEAD_PAD
        k_ref[:, a:b] = (kn[:, a:b] + kpe).astype(BF16)
    vt = _dot_nt(wvt_ref[...], ckvn)
    vrow = lax.broadcasted_iota(jnp.int32, vt.shape, 0) % HEAD_PAD
    vt_ref[...] = jnp.where(vrow == MLA_V, 1.0, vt).astype(BF16)

    cv = _dot(u, wa_ref[:, C_CONV:C_GLA])
    cb_ref[...] = cv[:, 0:CONV_CH]
    y_ref[...] = cv[:, CONV_CH:2 * CONV_CH] * cv[:, 2 * CONV_CH:3 * CONV_CH]

    g = _dot(u, wa_ref[:, C_GLA:C_GATES])
    gq = g[:, 0:GLA_K] * (GLA_DK ** -0.5)
    gk = g[:, GLA_K:2 * GLA_K]
    gv = g[:, 2 * GLA_K:2 * GLA_K + GLA_V]
    gr = g[:, 2 * GLA_K + GLA_V:2 * GLA_K + 2 * GLA_V]
    low = g[:, 2 * GLA_K + 2 * GLA_V:]
    pre = _dot(low.astype(BF16), wgate_ref[...]) + bgate_ref[...]
    logg = (jnp.minimum(pre, 0.0) - jnp.log(1.0 + jnp.exp(-jnp.abs(pre)))) * (1.0 / GLA_TAU)
    lf, lb = logg[:, 0:GLA_K], logg[:, GLA_K:2 * GLA_K]
    tm = lf.shape[0]
    ri = lax.broadcasted_iota(jnp.int32, (tm, tm), 0)
    ci = lax.broadcasted_iota(jnp.int32, (tm, tm), 1)
    same = (ri // GLA_CHUNK) == (ci // GLA_CHUNK)
    m_low = jnp.where(same & (ci <= ri), 1.0, 0.0).astype(BF16)
    m_up = jnp.where(same & (ci >= ri), 1.0, 0.0).astype(BF16)
    pieces = jnp.concatenate(_split_hi_lo(lf) + _split_hi_lo(lb), axis=1)
    pm = _dot(m_low, pieces)
    pu = _dot(m_up, pieces)
    pre_f = pm[:, 0:GLA_K] + pm[:, GLA_K:2 * GLA_K]
    pre_b = pm[:, 2 * GLA_K:3 * GLA_K] + pm[:, 3 * GLA_K:4 * GLA_K]
    suf_f = pu[:, 0:GLA_K] + pu[:, GLA_K:2 * GLA_K]
    suf_b = pu[:, 2 * GLA_K:3 * GLA_K] + pu[:, 3 * GLA_K:4 * GLA_K]
    gl_ref[:, 0:GLA_K] = (gq * jnp.exp(pre_f)).astype(BF16)
    gl_ref[:, GLA_K:2 * GLA_K] = (gk * jnp.exp(-pre_f)).astype(BF16)
    gl_ref[:, 2 * GLA_K:3 * GLA_K] = (gk * jnp.exp(suf_f - lf)).astype(BF16)
    gl_ref[:, 3 * GLA_K:4 * GLA_K] = (gq * jnp.exp(suf_b)).astype(BF16)
    gl_ref[:, 4 * GLA_K:5 * GLA_K] = (gk * jnp.exp(-suf_b)).astype(BF16)
    gl_ref[:, 5 * GLA_K:6 * GLA_K] = (gk * jnp.exp(pre_b - lb)).astype(BF16)
    gl_ref[:, 6 * GLA_K:6 * GLA_K + GLA_V] = gv.astype(BF16)
    dec_ref[:, 0:GLA_K] = jnp.exp(pre_f + suf_f - lf)
    dec_ref[:, GLA_K:2 * GLA_K] = jnp.exp(pre_b + suf_b - lb)
    sr_ref[...] = _silu(gr)

    for j in range(3):
        a, b = j * D_MODEL, (j + 1) * D_MODEL
        sg_ref[:, a:b] = _sigmoid(_dot(u, wa_ref[:, C_GATES + a:C_GATES + b])).astype(BF16)


def _in_proj(x, mods, l, rope_tab, wa, gq, wq, gkv, wk, wvt, wgate, bgate):
    bn, t, d = x.shape
    nt = t // TM
    tok = lambda w: pl.BlockSpec((None, TM, w), lambda b, i: (b, i, 0))
    outs = [(MLA_HEADS * HEAD_PAD, BF16), (MLA_HEADS * HEAD_PAD, BF16), None,
            (CONV_CH, F32), (CONV_CH, F32), (6 * GLA_K + GLA_V, BF16), (2 * GLA_K, F32), (GLA_V, F32),
            (3 * D_MODEL, BF16)]
    vt_rows = MLA_HEADS * HEAD_PAD
    vt_spec = pl.BlockSpec((None, vt_rows, TM), lambda b, i: (b, 0, i))
    vt_shape = jax.ShapeDtypeStruct((bn, vt_rows, t), BF16)
    consts = [wa, gq, wq, gkv, wk, wvt, wgate, bgate]
    return pl.pallas_call(
        _in_proj_kernel,
        grid=(bn, nt),
        in_specs=[tok(d),
                  pl.BlockSpec((None, None, 6, d), lambda b, i: (l, jnp.where(i == 0, bn, b), 0, 0)),
                  pl.BlockSpec((TM, 4 * LANES), lambda b, i: (i, 0))]
                 + [_const_spec(a.shape) for a in consts],
        out_specs=[vt_spec if o is None else tok(o[0]) for o in outs],
        out_shape=[vt_shape if o is None else jax.ShapeDtypeStruct((bn, t, o[0]), o[1]) for o in outs],
        compiler_params=_params("parallel", "parallel"),
        name="in_proj",
    )(x, mods, rope_tab, *consts)


def _attn_kernel(q_ref, k_ref, vt_ref, o_ref, s0_ref, s1_ref, *, ctx_len):
    s_refs = (s0_ref, s1_ref)
    kc = ATTN_KEY_CHUNK

    def scores(j, c, m):
        k = k_ref[c * kc:(c + 1) * kc, j * HEAD_PAD:(j + 1) * HEAD_PAD]
        st = _dot_nt(k, q_ref[:, j * HEAD_PAD:(j + 1) * HEAD_PAD])
        s_refs[j][c * kc:(c + 1) * kc, :] = st
        mc = jnp.max(st, axis=0, keepdims=True)
        return mc if m is None else jnp.maximum(m, mc)

    def weighted(j, c, m, acc):
        p = jnp.exp2(s_refs[j][c * kc:(c + 1) * kc, :] - m).astype(BF16)
        part = _dot(vt_ref[j * HEAD_PAD:(j + 1) * HEAD_PAD, c * kc:(c + 1) * kc], p)
        return part if acc is None else acc + part

    def attend(nk):
        nc = nk // kc
        m0 = None
        for c in range(nc):
            m0 = scores(0, c, m0)
        m1, acc0 = None, None
        for c in range(nc):
            acc0 = weighted(0, c, m0, acc0)
            m1 = scores(1, c, m1)
        acc1 = None
        for c in range(nc):
            acc1 = weighted(1, c, m1, acc1)
        outs = [a[0:MLA_V, :] / a[MLA_V:MLA_V + 1, :] for a in (acc0, acc1)]
        o_ref[...] = jnp.concatenate(outs, axis=0).T.astype(o_ref.dtype)

    is_ctx = pl.program_id(2) == 0
    pl.when(is_ctx)(lambda: attend(ctx_len))
    pl.when(jnp.logical_not(is_ctx))(lambda: attend(k_ref.shape[0]))


def _attention(q, k, vt):
    bn, t, _ = q.shape
    hp = MLA_HEADS // 2
    return pl.pallas_call(
        functools.partial(_attn_kernel, ctx_len=TM),
        grid=(bn, hp, t // TM),
        in_specs=[pl.BlockSpec((None, TM, 2 * HEAD_PAD), lambda b, h, i: (b, i, h)),
                  pl.BlockSpec((None, t, 2 * HEAD_PAD), lambda b, h, i: (b, 0, h)),
                  pl.BlockSpec((None, 2 * HEAD_PAD, t), lambda b, h, i: (b, h, 0))],
        out_specs=pl.BlockSpec((None, TM, 2 * MLA_V), lambda b, h, i: (b, i, h)),
        out_shape=jax.ShapeDtypeStruct((bn, t, MLA_HEADS * MLA_V), BF16),
        scratch_shapes=[pltpu.VMEM((t, TM), F32)] * 2,
        compiler_params=_params("parallel", "parallel", "arbitrary"),
        name="attention",
    )(q, k, vt)


def _gla_kernel(qf_ref, vf_ref, df_ref, qb_ref, vb_ref, db_ref, of_ref, ob_ref, sf_ref, sb_ref):
    @pl.when(pl.program_id(1) == 0)
    def _():
        sf_ref[...] = jnp.zeros_like(sf_ref)
        sb_ref[...] = jnp.zeros_like(sb_ref)

    c = GLA_CHUNK
    lane_k = lax.broadcasted_iota(jnp.int32, (c, GLA_K), 1) // GLA_DK
    lane_v = lax.broadcasted_iota(jnp.int32, (c, GLA_V), 1) // GLA_DV
    ai = lax.broadcasted_iota(jnp.int32, (c, GLA_HEADS * c), 0)
    aj = lax.broadcasted_iota(jnp.int32, (c, GLA_HEADS * c), 1) % c
    state_mask = (lax.broadcasted_iota(jnp.int32, (GLA_V, GLA_K), 0) // GLA_DV ==
                  lax.broadcasted_iota(jnp.int32, (GLA_V, GLA_K), 1) // GLA_DK)
    zero = jnp.zeros((), BF16)

    def chunk(g_ref, v_ref, d_ref, o_ref, s_ref, r, causal):
        rows = slice(r * c, (r + 1) * c)
        qd = g_ref[rows, 0:GLA_K]
        kd = g_ref[rows, GLA_K:2 * GLA_K]
        kend = g_ref[rows, 2 * GLA_K:3 * GLA_K]
        v = v_ref[rows, :]
        dec = d_ref[r * c:r * c + 1, :]
        kd_heads = jnp.concatenate([jnp.where(lane_k == h, kd, zero) for h in range(GLA_HEADS)], axis=0)
        v_heads = jnp.concatenate([jnp.where(lane_v == h, v, zero) for h in range(GLA_HEADS)], axis=0)
        att = jnp.where(causal, _dot_nt(qd, kd_heads), 0.0)
        st = s_ref[...]
        o_ref[rows, :] = _dot(att.astype(BF16), v_heads) + _dot_nt(qd, st.astype(BF16))
        s_ref[...] = st * dec + jnp.where(state_mask, _dot_tn(v, kend), 0.0)

    n_chunks = qf_ref.shape[0] // c
    for r in range(n_chunks):
        chunk(qf_ref, vf_ref, df_ref, of_ref, sf_ref, r, aj <= ai)
    for r in reversed(range(n_chunks)):
        chunk(qb_ref, vb_ref, db_ref, ob_ref, sb_ref, r, aj >= ai)


def _gla(gl, dec):
    bn, t, _ = gl.shape
    nt = t // TM
    fwd = lambda i: i
    bwd = lambda i: jnp.where(i == 0, 0, nt - i)
    spec = lambda w, order, col: pl.BlockSpec((None, TM, w), lambda b, i: (b, order(i), col))
    return pl.pallas_call(
        _gla_kernel,
        grid=(bn, nt),
        in_specs=[spec(3 * GLA_K, fwd, 0), spec(GLA_V, fwd, 3), spec(GLA_K, fwd, 0),
                  spec(3 * GLA_K, bwd, 1), spec(GLA_V, bwd, 3), spec(GLA_K, bwd, 1)],
        out_specs=[spec(GLA_V, fwd, 0), spec(GLA_V, bwd, 0)],
        out_shape=[jax.ShapeDtypeStruct((bn, t, GLA_V), F32)] * 2,
        scratch_shapes=[pltpu.VMEM((GLA_V, GLA_K), F32)] * 2,
        compiler_params=_params("parallel", "arbitrary"),
        name="gla",
    )(gl, gl, dec, gl, gl, dec)


def _finish_kernel(x_ref, mod_ref, y_ref, yp_ref, yn_ref, cb_ref, sr_ref, sg_ref, att_ref, of_ref, ob_ref,
                   wmla_ref, cw_ref, cbias_ref, wconv_ref, gnorm_ref, wgla_ref, wout_ref, lng_ref, lnb_ref,
                   o_ref, *, alpha):
    i = pl.program_id(1)
    nt = pl.num_programs(1)
    tm = y_ref.shape[0]
    y = y_ref[...]
    has_prev = jnp.logical_and(i != 0, i != 1)
    has_next = jnp.logical_and(i != 0, i != nt - 1)
    prev_row = jnp.where(has_prev, yp_ref[7:8, :], 0.0)
    next_row = jnp.where(has_next, yn_ref[0:1, :], 0.0)
    row = lax.broadcasted_iota(jnp.int32, y.shape, 0)
    y_m1 = jnp.where(row == 0, prev_row, pltpu.roll(y, 1, axis=0))
    y_p1 = jnp.where(row == tm - 1, next_row, pltpu.roll(y, tm - 1, axis=0))
    conv = cbias_ref[...] + y_m1 * cw_ref[0:1, :] + y * cw_ref[1:2, :] + y_p1 * cw_ref[2:3, :]
    conv = cb_ref[...] * conv

    o = of_ref[...] + ob_ref[...]
    gi = lax.broadcasted_iota(jnp.int32, (GLA_V, GLA_V), 0) // GLA_DV
    gj = lax.broadcasted_iota(jnp.int32, (GLA_V, GLA_V), 1) // GLA_DV
    grp = jnp.where(gi == gj, 1.0 / GLA_DV, 0.0).astype(BF16)
    hi, lo = _split_hi_lo(o * o)
    ms = _dot(hi, grp) + _dot(lo, grp)
    gla = o * lax.rsqrt(ms + NORM_EPS) * gnorm_ref[...] * sr_ref[...]

    d = x_ref.shape[1]
    merged = (sg_ref[:, 0:d].astype(F32) * _dot(att_ref[...], wmla_ref[...])
              + sg_ref[:, d:2 * d].astype(F32) * _dot(conv.astype(BF16), wconv_ref[...])
              + sg_ref[:, 2 * d:3 * d].astype(F32) * _dot(gla.astype(BF16), wgla_ref[...]))
    m = _dot(merged.astype(BF16), wout_ref[...])
    o_ref[...] = _layer_norm(alpha * x_ref[...] + mod_ref[2:3, :] * m, lng_ref[...], lnb_ref[...])


def _finish(x, mods, l, y, cb, sr, sg, att, o_f, o_b, wmla, cw, cbias, wconv, gnorm, wgla, wout, lng, lnb, alpha):
    bn, t, d = x.shape
    nt = t // TM
    tok = lambda w: pl.BlockSpec((None, TM, w), lambda b, i: (b, i, 0))
    rows8 = TM // 8
    consts = [wmla, cw, cbias, wconv, gnorm, wgla, wout, lng, lnb]
    return pl.pallas_call(
        functools.partial(_finish_kernel, alpha=alpha),
        grid=(bn, nt),
        in_specs=[tok(d),
                  pl.BlockSpec((None, None, 6, d), lambda b, i: (l, jnp.where(i == 0, bn, b), 0, 0)),
                  tok(CONV_CH),
                  pl.BlockSpec((None, 8, CONV_CH), lambda b, i: (b, jnp.maximum(i * rows8 - 1, 0), 0)),
                  pl.BlockSpec((None, 8, CONV_CH), lambda b, i: (b, jnp.minimum((i + 1) * rows8, nt * rows8 - 1), 0)),
                  tok(CONV_CH), tok(GLA_V), tok(3 * d), tok(MLA_HEADS * MLA_V), tok(GLA_V), tok(GLA_V)]
                 + [_const_spec(a.shape) for a in consts],
        out_specs=tok(d),
        out_shape=jax.ShapeDtypeStruct((bn, t, d), F32),
        compiler_params=_params("parallel", "parallel"),
        name="finish",
    )(x, mods, y, y, y, cb, sr, sg, att, o_f, o_b, *consts)


def _router_kernel(x_ref, mod_ref, rwh_ref, rwl_ref, rb_ref, u_ref, br_ref, cnt_ref, base_ref):
    first = jnp.logical_and(pl.program_id(0) == 0, pl.program_id(1) == 0)

    @pl.when(first)
    def _():
        base_ref[...] = jnp.zeros_like(base_ref)

    tm = x_ref.shape[0]
    u = x_ref[...] * (1.0 + mod_ref[4:5, :]) + mod_ref[3:4, :]
    hi, lo = _split_hi_lo(u)
    for j in range(TOKEN_ROWS):
        u_ref[pl.ds(j, tm, stride=XS_ROWS), :] = u[:, j * LANES:(j + 1) * LANES]
    logits = _dot(hi, rwh_ref[...]) + _dot(lo, rwh_ref[...]) + _dot(hi, rwl_ref[...])
    s = _sigmoid(logits.T[0:N_EXPERTS, :])
    sel = s + rb_ref[...]

    sel_e = [sel[e:e + 1, :] for e in range(N_EXPERTS)]
    chosen, gscore = [], []
    for g in range(N_GROUPS):
        members = list(range(g * PER_GROUP, (g + 1) * PER_GROUP))
        picked = []
        for a in members:
            beaten = jnp.zeros((1, tm), F32)
            for b in members:
                if b != a:
                    wins = (sel_e[b] >= sel_e[a]) if b < a else (sel_e[b] > sel_e[a])
                    beaten = beaten + jnp.where(wins, 1.0, 0.0)
            picked.append(beaten < 2.0)
        chosen.append(picked)
        score = jnp.zeros((1, tm), F32)
        for a, p in zip(members, picked):
            score = score + jnp.where(p, sel_e[a], 0.0)
        gscore.append(score)
    best = []
    for g in range(N_GROUPS):
        ok = jnp.ones((1, tm), jnp.bool_)
        for h in range(N_GROUPS):
            if h < g:
                ok = jnp.logical_and(ok, gscore[g] > gscore[h])
            elif h > g:
                ok = jnp.logical_and(ok, gscore[g] >= gscore[h])
        best.append(ok)

    zero = jnp.zeros((1, tm), F32)
    w_lo, w_hi = zero, zero
    bucket_rows = []
    for g in range(N_GROUPS):
        taken = [jnp.logical_and(best[g], p) for p in chosen[g]]
        seen = jnp.zeros((1, tm), jnp.bool_)
        for a in range(PER_GROUP):
            sa = s[g * PER_GROUP + a:g * PER_GROUP + a + 1, :]
            w_lo = w_lo + jnp.where(jnp.logical_and(taken[a], jnp.logical_not(seen)), sa, 0.0)
            w_hi = w_hi + jnp.where(jnp.logical_and(taken[a], seen), sa, 0.0)
            seen = jnp.logical_or(seen, taken[a])
        for a in range(PER_GROUP):
            for b in range(a + 1, PER_GROUP):
                bucket_rows.append(jnp.where(jnp.logical_and(taken[a], taken[b]), 1.0, 0.0))
    total = w_lo + w_hi
    w_lo, w_hi = w_lo / total, w_hi / total
    wts = jnp.concatenate([w_lo, w_hi, jnp.zeros((LANES - 2, tm), F32)], axis=0)
    u_ref[pl.ds(TOKEN_ROWS, tm, stride=XS_ROWS), :] = wts.T
    for j in range(TOKEN_ROWS + 1, XS_ROWS):
        u_ref[pl.ds(j, tm, stride=XS_ROWS), :] = jnp.zeros((tm, LANES), F32)

    onehot = jnp.concatenate(bucket_rows + [jnp.zeros((BUCKET_ROWS - N_BUCKETS, tm), F32)], axis=0)
    ri = lax.broadcasted_iota(jnp.int32, (tm, tm), 0)
    ci = lax.broadcasted_iota(jnp.int32, (tm, tm), 1)
    before = _dot(onehot.astype(BF16), jnp.where(ri < ci, 1.0, 0.0).astype(BF16))
    base = base_ref[:, 0:1]
    rank = jnp.sum(onehot * (before + base), axis=0, keepdims=True)
    bidx = lax.broadcasted_iota(jnp.int32, onehot.shape, 0).astype(F32)
    bucket = jnp.sum(onehot * bidx, axis=0, keepdims=True)
    br_ref[...] = jnp.concatenate([bucket, rank, jnp.zeros((6, tm), F32)], axis=0).astype(jnp.int32)
    new_base = base + jnp.sum(onehot, axis=1, keepdims=True)
    base_ref[...] = jnp.broadcast_to(new_base, base_ref.shape)
    cnt_ref[...] = jnp.broadcast_to(new_base, cnt_ref.shape).astype(jnp.int32)


def _router(x1, mods, l, rw_hi, rw_lo, rb):
    bn, t, d = x1.shape
    nt = t // TM
    return pl.pallas_call(
        _router_kernel,
        grid=(bn, nt),
        in_specs=[pl.BlockSpec((None, TM, d), lambda b, i: (b, i, 0)),
                  pl.BlockSpec((None, None, 6, d), lambda b, i: (l, jnp.where(i == 0, bn, b), 0, 0)),
                  _const_spec(rw_hi.shape), _const_spec(rw_lo.shape), _const_spec(rb.shape)],
        out_specs=[pl.BlockSpec((TM * XS_ROWS, LANES), lambda b, i: (b * nt + i, 0)),
                   pl.BlockSpec((None, None, 8, TM), lambda b, i: (b, i, 0, 0)),
                   pl.BlockSpec((BUCKET_ROWS, LANES), lambda b, i: (0, 0))],
        out_shape=[jax.ShapeDtypeStruct((bn * t * XS_ROWS, LANES), F32),
                   jax.ShapeDtypeStruct((bn, nt, 8, TM), jnp.int32),
                   jax.ShapeDtypeStruct((BUCKET_ROWS, LANES), jnp.int32)],
        scratch_shapes=[pltpu.VMEM((BUCKET_ROWS, LANES), F32)],
        compiler_params=_params("arbitrary", "arbitrary"),
        name="router",
    )(x1, mods, rw_hi, rw_lo, rb)


def _place_kernel(off_ref, bucket_ref, rank_ref, pos_ref):
    bucket = bucket_ref[...]
    pos = rank_ref[...]
    for b in range(N_BUCKETS):
        pos = pos + jnp.where(bucket == b, off_ref[b], 0)
    pos_ref[...] = pos


def _place(offsets, bucket, rank):
    return pl.pallas_call(
        _place_kernel,
        in_specs=[pl.BlockSpec(memory_space=pltpu.SMEM), pl.BlockSpec(memory_space=pltpu.VMEM),
                  pl.BlockSpec(memory_space=pltpu.VMEM)],
        out_specs=pl.BlockSpec(memory_space=pltpu.VMEM),
        out_shape=jax.ShapeDtypeStruct(bucket.shape, jnp.int32),
        name="place",
    )(offsets, bucket, rank)


def _slot_copy(src_ref, src_slot, dst_ref, dst_slot, sem, rows):
    return pltpu.make_async_copy(src_ref.at[pl.ds(pl.multiple_of(src_slot * rows, rows), rows)],
                                 dst_ref.at[pl.ds(pl.multiple_of(dst_slot * rows, rows), rows)], sem)


def _scatter_kernel(pos_ref, u_ref, init_ref, xs_ref, sem):
    del init_ref
    tm = u_ref.shape[0] // XS_ROWS

    def issue(r, carry):
        _slot_copy(u_ref, r, xs_ref, pos_ref[0, r], sem, XS_ROWS).start()
        return carry

    lax.fori_loop(0, tm, issue, 0, unroll=ISSUE_UNROLL)
    pltpu.make_async_copy(u_ref, xs_ref.at[pl.ds(0, tm * XS_ROWS)], sem).wait()


def _scatter(pos, u_tok, init):
    n_tiles = pos.shape[0]
    rows = TM * XS_ROWS
    return pl.pallas_call(
        _scatter_kernel,
        grid=(n_tiles,),
        in_specs=[pl.BlockSpec((None, 1, TM), lambda i: (i, 0, 0), memory_space=pltpu.SMEM),
                  pl.BlockSpec((rows, LANES), lambda i: (i, 0)),
                  pl.BlockSpec(memory_space=pl.ANY)],
        out_specs=pl.BlockSpec(memory_space=pl.ANY),
        out_shape=jax.ShapeDtypeStruct(init.shape, init.dtype),
        scratch_shapes=[pltpu.SemaphoreType.DMA],
        input_output_aliases={2: 0},
        compiler_params=_params("arbitrary"),
        name="scatter",
    )(pos.reshape(n_tiles, 1, TM), u_tok, init)


def _expert_kernel(e1_ref, e2_ref, used_ref, xs_ref, wg1_ref, wu1_ref, wd1_ref, wg2_ref, wu2_ref, wd2_ref, y_ref):
    del e1_ref, e2_ref
    live = pl.program_id(0) < used_ref[0]
    tm = xs_ref.shape[0] // XS_ROWS

    @pl.when(live)
    def _():
        x = jnp.concatenate([xs_ref[pl.ds(j, tm, stride=XS_ROWS), :].astype(BF16) for j in range(TOKEN_ROWS)], axis=1)
        wts = xs_ref[pl.ds(TOKEN_ROWS, tm, stride=XS_ROWS), :]
        h1 = _silu(_dot(x, wg1_ref[...])) * _dot(x, wu1_ref[...])
        h2 = _silu(_dot(x, wg2_ref[...])) * _dot(x, wu2_ref[...])
        y = wts[:, 0:1] * _dot(h1.astype(BF16), wd1_ref[...]) + wts[:, 1:2] * _dot(h2.astype(BF16), wd2_ref[...])
        for j in range(TOKEN_ROWS):
            y_ref[pl.ds(j, tm, stride=TOKEN_ROWS), :] = y[:, j * LANES:(j + 1) * LANES]

    @pl.when(jnp.logical_not(live))
    def _():
        y_ref[...] = jnp.zeros_like(y_ref)


def _experts(tile_e1, tile_e2, n_used, xs, wg, wu, wd, l):
    n_tiles = xs.shape[0] // (TM * XS_ROWS)
    d, de = wg.shape[1], wg.shape[2]
    up = lambda tbl: pl.BlockSpec((None, d, de), lambda i, e1, e2, n: (l * N_EXPERTS + tbl(e1, e2)[i], 0, 0))
    down = lambda tbl: pl.BlockSpec((None, de, d), lambda i, e1, e2, n: (l * N_EXPERTS + tbl(e1, e2)[i], 0, 0))
    first = lambda e1, e2: e1
    second = lambda e1, e2: e2
    return pl.pallas_call(
        _expert_kernel,
        grid_spec=pltpu.PrefetchScalarGridSpec(
            num_scalar_prefetch=3,
            grid=(n_tiles,),
            in_specs=[pl.BlockSpec((TM * XS_ROWS, LANES), lambda i, e1, e2, n: (i, 0)),
                      up(first), up(first), down(first), up(second), up(second), down(second)],
            out_specs=pl.BlockSpec((TM * TOKEN_ROWS, LANES), lambda i, e1, e2, n: (i, 0)),
        ),
        out_shape=jax.ShapeDtypeStruct((n_tiles * TM * TOKEN_ROWS, LANES), F32),
        compiler_params=_params("arbitrary"),
        name="experts",
    )(tile_e1, tile_e2, n_used, xs, wg, wu, wd, wg, wu, wd)


def _ffn_out_kernel(pos_ref, nxt_ref, x_ref, mod_ref, ys_ref, lng_ref, lnb_ref, o_ref, buf_ref, sem, *, alpha):
    tm = x_ref.shape[0]
    step = pl.program_id(0) * pl.num_programs(1) + pl.program_id(1)
    n_steps = pl.num_programs(0) * pl.num_programs(1)
    slot = step % 2

    def gather(idx_ref, s):
        def issue(r, carry):
            _slot_copy(ys_ref, idx_ref[0, r], buf_ref.at[s], r, sem.at[s], TOKEN_ROWS).start()
            return carry
        lax.fori_loop(0, tm, issue, 0, unroll=ISSUE_UNROLL)

    pl.when(step == 0)(lambda: gather(pos_ref, 0))
    pl.when(step + 1 < n_steps)(lambda: gather(nxt_ref, 1 - slot))
    cur = buf_ref.at[slot]
    pltpu.make_async_copy(ys_ref.at[pl.ds(0, tm * TOKEN_ROWS)], cur, sem.at[slot]).wait()
    f = jnp.concatenate([cur[pl.ds(j, tm, stride=TOKEN_ROWS), :] for j in range(TOKEN_ROWS)], axis=1)
    o_ref[...] = _layer_norm(alpha * x_ref[...] + mod_ref[5:6, :] * f, lng_ref[...], lnb_ref[...])


def _ffn_out(pos, x1, mods, l, ys, lng, lnb, alpha):
    bn, t, d = x1.shape
    nt = t // TM
    last = bn * nt - 1
    pos3 = pos.reshape(bn * nt, 1, TM)
    return pl.pallas_call(
        functools.partial(_ffn_out_kernel, alpha=alpha),
        grid=(bn, nt),
        in_specs=[pl.BlockSpec((None, 1, TM), lambda b, i: (b * nt + i, 0, 0), memory_space=pltpu.SMEM),
                  pl.BlockSpec((None, 1, TM), lambda b, i: (jnp.minimum(b * nt + i + 1, last), 0, 0),
                               memory_space=pltpu.SMEM),
                  pl.BlockSpec((None, TM, d), lambda b, i: (b, i, 0)),
                  pl.BlockSpec((None, None, 6, d), lambda b, i: (l, jnp.where(i == 0, bn, b), 0, 0)),
                  pl.BlockSpec(memory_space=pl.ANY),
                  _const_spec(lng.shape), _const_spec(lnb.shape)],
        out_specs=pl.BlockSpec((None, TM, d), lambda b, i: (b, i, 0)),
        out_shape=jax.ShapeDtypeStruct((bn, t, d), F32),
        scratch_shapes=[pltpu.VMEM((2, TM * TOKEN_ROWS, LANES), F32), pltpu.SemaphoreType.DMA((2,))],
        compiler_params=_params("arbitrary", "arbitrary"),
        name="ffn_out",
    )(pos3, pos3, x1, mods, ys, lng, lnb)


def _rope_tables(ctx_len, seq_len):
    t = np.arange(seq_len)
    half = MLA_ROPE // 2
    inv_freq = 1.0 / (ROPE_BASE ** (np.arange(0, half, 2, dtype=np.float32) / half))
    ang_r = (t // GRID_W).astype(np.float32)[:, None] * inv_freq[None, :]
    ang_c = (t % GRID_W).astype(np.float32)[:, None] * inv_freq[None, :]
    ang = np.concatenate([ang_r, ang_r, ang_c, ang_c], axis=-1).astype(np.float32)
    cos = np.concatenate([np.ones((ctx_len, MLA_ROPE), np.float32), np.cos(ang)], axis=0)
    sin = np.concatenate([np.zeros((ctx_len, MLA_ROPE), np.float32), np.sin(ang)], axis=0)
    n = ctx_len + seq_len
    c_tab = np.zeros((n, LANES), np.float32)
    s_tab = np.zeros((n, LANES), np.float32)
    c_tab[:, :MLA_NOPE] = 1.0
    c_tab[:, MLA_NOPE:QK_DIM] = cos
    s_tab[:, MLA_NOPE:QK_DIM] = sin
    scale = np.float32(QK_DIM ** -0.5 * np.log2(np.e))
    return jnp.asarray(np.concatenate([c_tab * scale, s_tab * scale, c_tab, s_tab], axis=1))


def _rotate_cols(w):
    q = MLA_ROPE // 4
    w1, w2, w3, w4 = (w[..., i * q:(i + 1) * q] for i in range(4))
    return jnp.concatenate([-w2, w1, -w4, w3], axis=-1)


def _prep_layer_weights(w_in, mla_wq_b, mla_wkv_b, gla_w_gate, gla_b_gate):
    depth, d, _ = w_in.shape
    splits = np.cumsum([0, MLA_Q_RANK, MLA_KV_RANK, MLA_ROPE, CONV_CH, CONV_CH, CONV_CH, GLA_K, GLA_K, GLA_V,
                        GLA_V, GLA_GATE_RANK, GLA_GATE_RANK, 3 * D_MODEL])
    seg = [w_in[:, :, splits[i]:splits[i + 1]] for i in range(13)]
    zeros = lambda *s: jnp.zeros(s, F32)
    pad_rope = lambda w: jnp.concatenate([zeros(depth, d, MLA_NOPE), w, zeros(depth, d, LANES - QK_DIM)], axis=-1)
    wa = jnp.concatenate(
        [seg[0], seg[1], pad_rope(seg[2]), pad_rope(_rotate_cols(seg[2])), seg[3], seg[4], seg[5], seg[6], seg[7],
         seg[8], seg[9], seg[10], seg[11], zeros(depth, d, LANES - 2 * GLA_GATE_RANK), seg[12]], axis=-1).astype(BF16)

    r = mla_wq_b.shape[1]
    wq = mla_wq_b.reshape(depth, r, MLA_HEADS, QK_DIM)
    q_main = jnp.concatenate([wq, zeros(depth, r, MLA_HEADS, HEAD_PAD - QK_DIM)], axis=-1)
    q_rot = jnp.concatenate([zeros(depth, r, MLA_HEADS, MLA_NOPE), _rotate_cols(wq[..., MLA_NOPE:]),
                             zeros(depth, r, MLA_HEADS, HEAD_PAD - QK_DIM)], axis=-1)
    wq2 = jnp.concatenate([q_main.reshape(depth, r, -1), q_rot.reshape(depth, r, -1)], axis=-1).astype(BF16)

    rk = mla_wkv_b.shape[1]
    wkv = mla_wkv_b.reshape(depth, rk, MLA_HEADS, MLA_NOPE + MLA_V)
    k_part = jnp.concatenate([wkv[..., :MLA_NOPE], zeros(depth, rk, MLA_HEADS, HEAD_PAD - MLA_NOPE)], axis=-1)
    wk = k_part.reshape(depth, rk, -1).astype(BF16)
    v_part = jnp.concatenate([wkv[..., MLA_NOPE:], zeros(depth, rk, MLA_HEADS, HEAD_PAD - MLA_V)], axis=-1)
    wvt = jnp.swapaxes(v_part.reshape(depth, rk, -1), 1, 2).astype(BF16)

    gr = GLA_GATE_RANK
    wgate = jnp.zeros((depth, LANES, 2 * GLA_K), F32)
    wgate = wgate.at[:, 0:gr, 0:GLA_K].set(gla_w_gate[:, 0]).at[:, gr:2 * gr, GLA_K:].set(gla_w_gate[:, 1])
    bgate = gla_b_gate.reshape(depth, 1, 2 * GLA_K)
    return wa, wq2, wk, wvt, wgate.astype(BF16), bgate


def _bucket_tables(counts, n_sorted_tiles):
    padded = ((counts + TM - 1) // TM) * TM
    ends = jnp.cumsum(padded)
    offsets = ends - padded
    n_used = (ends[-1] // TM).astype(jnp.int32)
    tile_start = jnp.arange(n_sorted_tiles, dtype=jnp.int32) * TM
    tile_bucket = jnp.sum((tile_start[:, None] >= ends[None, :]).astype(jnp.int32), axis=1)
    last_bucket = jnp.sum((jnp.maximum(ends[-1] - TM, 0) >= ends).astype(jnp.int32))
    tile_bucket = jnp.where(tile_start < ends[-1], tile_bucket, last_bucket)
    pair_lo = np.array([a for a in range(PER_GROUP) for b in range(a + 1, PER_GROUP)], np.int32)
    pair_hi = np.array([b for a in range(PER_GROUP) for b in range(a + 1, PER_GROUP)], np.int32)
    group = tile_bucket // N_PAIRS
    pair = tile_bucket % N_PAIRS
    e1 = group * PER_GROUP + jnp.asarray(pair_lo)[pair]
    e2 = group * PER_GROUP + jnp.asarray(pair_hi)[pair]
    return offsets.astype(jnp.int32), e1.astype(jnp.int32), e2.astype(jnp.int32), n_used.reshape(1)


def kernel(x, c, ctx, c_ctx, w_mod, b_mod, w_in, mla_q_norm, mla_wq_b, mla_kv_norm, mla_wkv_b, mla_w_o, conv_w,
           conv_b, conv_w_o, gla_w_gate, gla_b_gate, gla_norm, gla_w_o, w_out, ln1_g, ln1_b, router_w, router_b,
           exp_wg, exp_wu, exp_wd, ln2_g, ln2_b):
    bn, seq_len, d = x.shape
    ctx_len = ctx.shape[1]
    depth = w_in.shape[0]
    assert ctx_len == TM and seq_len % TM == 0 and d == D_MODEL and bn < 8
    alpha = (2 * depth) ** 0.25
    t = ctx_len + seq_len
    n_tok = bn * t
    n_sorted_tiles = n_tok // TM + N_BUCKETS
    n_sorted = n_sorted_tiles * TM

    cond = jnp.concatenate([c, c_ctx[None, :], jnp.zeros((7 - bn, d), F32)], axis=0)
    mods = _modulation(cond, w_mod, b_mod).reshape(depth, 8, 6, d)
    rope_tab = _rope_tables(ctx_len, seq_len)
    wa, wq2, wk, wvt, wgate, bgate = _prep_layer_weights(w_in, mla_wq_b, mla_wkv_b, gla_w_gate, gla_b_gate)
    row = lambda a: a.reshape(depth, 1, a.shape[-1])
    gq, gkv, cbias, gnorm = row(mla_q_norm), row(mla_kv_norm), row(conv_b), row(gla_norm)
    lng1, lnb1, lng2, lnb2 = row(ln1_g), row(ln1_b), row(ln2_g), row(ln2_b)
    wmla, wconv, wgla, wout = (a.astype(BF16) for a in (mla_w_o, conv_w_o, gla_w_o, w_out))
    rw = jnp.concatenate([router_w, jnp.zeros((d, LANES - N_EXPERTS), F32)], axis=1)
    rw_hi = rw.astype(BF16)
    rw_lo = (rw - rw_hi.astype(F32)).astype(BF16)
    rb = router_b.reshape(N_EXPERTS, 1)
    de = exp_wg.shape[-1]
    wg = exp_wg.astype(BF16).reshape(depth * N_EXPERTS, d, de)
    wu = exp_wu.astype(BF16).reshape(depth * N_EXPERTS, d, de)
    wd = exp_wd.astype(BF16).reshape(depth * N_EXPERTS, de, d)

    xt = jnp.concatenate([ctx, x], axis=1)
    xs = jnp.zeros((n_sorted * XS_ROWS, LANES), F32)
    for l in range(depth):
        q, k, vt, y, cb, gl, dec, sr, sg = _in_proj(xt, mods, l, rope_tab, wa[l], gq[l], wq2[l], gkv[l], wk[l],
                                                    wvt[l], wgate[l], bgate[l])
        att = _attention(q, k, vt)
        o_f, o_b = _gla(gl, dec)
        x1 = _finish(xt, mods, l, y, cb, sr, sg, att, o_f, o_b, wmla[l], conv_w[l], cbias[l], wconv[l], gnorm[l],
                     wgla[l], wout[l], lng1[l], lnb1[l], alpha)
        u_ext, br, counts = _router(x1, mods, l, rw_hi, rw_lo, rb)
        offsets, tile_e1, tile_e2, n_used = _bucket_tables(counts[:N_BUCKETS, 0], n_sorted_tiles)
        br = br.reshape(bn * (t // TM), 8, TM)
        pos = _place(offsets, br[:, 0, :], br[:, 1, :])
        xs = _scatter(pos, u_ext, xs)
        ys = _experts(tile_e1, tile_e2, n_used, xs, wg, wu, wd, l)
        xt = _ffn_out(pos, x1, mods, l, ys, lng2[l], lnb2[l], alpha)
    return xt[:, ctx_len:, :]
```

```python
import functools

import numpy as np
import jax
import jax.numpy as jnp
from jax import lax
from jax.experimental import pallas as pl
from jax.experimental.pallas import tpu as pltpu

D_MODEL = 1024
GRID_W = 64
MLA_HEADS = 8
MLA_NOPE = 64
MLA_ROPE = 32
MLA_V = 64
MLA_Q_RANK = 256
MLA_KV_RANK = 128
ROPE_BASE = 10000.0
CONV_CH = 256
GLA_HEADS = 4
GLA_DK = 32
GLA_DV = 64
GLA_GATE_RANK = 16
GLA_TAU = 16.0
GLA_CHUNK = 64
N_EXPERTS = 16
N_GROUPS = 4
PER_GROUP = N_EXPERTS // N_GROUPS
D_EXPERT = 512
NORM_EPS = 1e-6
F32 = jnp.float32
BF16 = jnp.bfloat16

LANES = 128
TM = 256
HEAD_PAD = 128
QK_DIM = MLA_NOPE + MLA_ROPE
GLA_K = GLA_HEADS * GLA_DK
GLA_V = GLA_HEADS * GLA_DV
N_PAIRS = PER_GROUP * (PER_GROUP - 1) // 2
N_BUCKETS = N_GROUPS * N_PAIRS
BUCKET_ROWS = 32
TOKEN_ROWS = D_MODEL // LANES
XS_ROWS = 2 * TOKEN_ROWS
ISSUE_UNROLL = 8
ATTN_KEY_CHUNK = 256
VMEM_LIMIT = 56 * 1024 * 1024

C_CQ = 0
C_CKV = C_CQ + MLA_Q_RANK
C_KPE = C_CKV + MLA_KV_RANK
C_CONV = C_KPE + 2 * LANES
C_GLA = C_CONV + 3 * CONV_CH
C_GATES = C_GLA + 2 * GLA_K + 2 * GLA_V + LANES
W_A_COLS = C_GATES + 3 * D_MODEL


def _params(*sem):
    return pltpu.CompilerParams(dimension_semantics=sem, vmem_limit_bytes=VMEM_LIMIT)


def _const_spec(shape):
    n = len(shape)
    return pl.BlockSpec(shape, lambda *_: (0,) * n, pipeline_mode=pl.Buffered(1))


def _dot(a, b):
    return jnp.dot(a, b, preferred_element_type=F32)


def _dot_nt(a, b):
    return lax.dot_general(a, b, (((1,), (1,)), ((), ())), preferred_element_type=F32)


def _dot_tn(a, b):
    return lax.dot_general(a, b, (((0,), (0,)), ((), ())), preferred_element_type=F32)


def _split_hi_lo(x):
    hi = x.astype(BF16)
    lo = (x - hi.astype(F32)).astype(BF16)
    return hi, lo


def _sigmoid(x):
    return 1.0 / (1.0 + jnp.exp(-x))


def _silu(x):
    return x * _sigmoid(x)


def _layer_norm(v, g, b):
    mu = jnp.mean(v, axis=-1, keepdims=True)
    d = v - mu
    var = jnp.mean(d * d, axis=-1, keepdims=True)
    return d * lax.rsqrt(var + NORM_EPS) * g + b


def _rms(v, g):
    return v * lax.rsqrt(jnp.mean(v * v, axis=-1, keepdims=True) + NORM_EPS) * g


def _mod_kernel(c_ref, w_ref, b_ref, o_ref):
    o_ref[...] = _dot(_silu(c_ref[...]).astype(BF16), w_ref[...].astype(BF16)) + b_ref[...]


def _modulation(cond, w_mod, b_mod):
    depth, d, n = w_mod.shape
    tn = n // 4
    return pl.pallas_call(
        _mod_kernel,
        grid=(depth, n // tn),
        in_specs=[pl.BlockSpec((8, d), lambda l, j: (0, 0)),
                  pl.BlockSpec((None, d, tn), lambda l, j: (l, 0, j)),
                  pl.BlockSpec((None, 1, tn), lambda l, j: (l, 0, j))],
        out_specs=pl.BlockSpec((None, 8, tn), lambda l, j: (l, 0, j)),
        out_shape=jax.ShapeDtypeStruct((depth, 8, n), F32),
        compiler_params=_params("parallel", "parallel"),
        name="modulation",
    )(cond, w_mod, b_mod.reshape(depth, 1, n))


def _in_proj_kernel(x_ref, mod_ref, rope_ref, wa_ref, gq_ref, wq_ref, gkv_ref, wk_ref, wvt_ref, wgate_ref, bgate_ref,
                    q_ref, k_ref, vt_ref, y_ref, cb_ref, gl_ref, dec_ref, sr_ref, sg_ref):
    u = (x_ref[...] * (1.0 + mod_ref[1:2, :]) + mod_ref[0:1, :]).astype(BF16)
    cq_tab, sq_tab = rope_ref[:, 0:LANES], rope_ref[:, LANES:2 * LANES]
    ck_tab, sk_tab = rope_ref[:, 2 * LANES:3 * LANES], rope_ref[:, 3 * LANES:4 * LANES]

    cq = _dot(u, wa_ref[:, C_CQ:C_CKV])
    q2 = _dot(_rms(cq, gq_ref[...]).astype(BF16), wq_ref[...])
    nq = MLA_HEADS * HEAD_PAD
    for h in range(MLA_HEADS):
        a, b = h * HEAD_PAD, (h + 1) * HEAD_PAD
        q_ref[:, a:b] = (q2[:, a:b] * cq_tab + q2[:, nq + a:nq + b] * sq_tab).astype(BF16)

    ckv = _dot(u, wa_ref[:, C_CKV:C_KPE])
    ckvn = _rms(ckv, gkv_ref[...]).astype(BF16)
    kn = _dot(ckvn, wk_ref[...])
    kpe2 = _dot(u, wa_ref[:, C_KPE:C_CONV])
    kpe = kpe2[:, 0:LANES] * ck_tab + kpe2[:, LANES:2 * LANES] * sk_tab
    for h in range(MLA_HEADS):
        a, b = h * HEAD_PAD, (h + 1) * HEAD_PAD
        k_ref[:, a:b] = (kn[:, a:b] + kpe).astype(BF16)
    vt = _dot_nt(wvt_ref[...], ckvn)
    vrow = lax.broadcasted_iota(jnp.int32, vt.shape, 0) % HEAD_PAD
    vt_ref[...] = jnp.where(vrow == MLA_V, 1.0, vt).astype(BF16)

    cv = _dot(u, wa_ref[:, C_CONV:C_GLA])
    cb_ref[...] = cv[:, 0:CONV_CH]
    y_ref[...] = cv[:, CONV_CH:2 * CONV_CH] * cv[:, 2 * CONV_CH:3 * CONV_CH]

    g = _dot(u, wa_ref[:, C_GLA:C_GATES])
    gq = g[:, 0:GLA_K] * (GLA_DK ** -0.5)
    gk = g[:, GLA_K:2 * GLA_K]
    gv = g[:, 2 * GLA_K:2 * GLA_K + GLA_V]
    gr = g[:, 2 * GLA_K + GLA_V:2 * GLA_K + 2 * GLA_V]
    low = g[:, 2 * GLA_K + 2 * GLA_V:]
    pre = _dot(low.astype(BF16), wgate_ref[...]) + bgate_ref[...]
    logg = (jnp.minimum(pre, 0.0) - jnp.log(1.0 + jnp.exp(-jnp.abs(pre)))) * (1.0 / GLA_TAU)
    lf, lb = logg[:, 0:GLA_K], logg[:, GLA_K:2 * GLA_K]
    tm = lf.shape[0]
    ri = lax.broadcasted_iota(jnp.int32, (tm, tm), 0)
    ci = lax.broadcasted_iota(jnp.int32, (tm, tm), 1)
    same = (ri // GLA_CHUNK) == (ci // GLA_CHUNK)
    m_low = jnp.where(same & (ci <= ri), 1.0, 0.0).astype(BF16)
    m_up = jnp.where(same & (ci >= ri), 1.0, 0.0).astype(BF16)
    pieces = jnp.concatenate(_split_hi_lo(lf) + _split_hi_lo(lb), axis=1)
    pm = _dot(m_low, pieces)
    pu = _dot(m_up, pieces)
    pre_f = pm[:, 0:GLA_K] + pm[:, GLA_K:2 * GLA_K]
    pre_b = pm[:, 2 * GLA_K:3 * GLA_K] + pm[:, 3 * GLA_K:4 * GLA_K]
    suf_f = pu[:, 0:GLA_K] + pu[:, GLA_K:2 * GLA_K]
    suf_b = pu[:, 2 * GLA_K:3 * GLA_K] + pu[:, 3 * GLA_K:4 * GLA_K]
    gl_ref[:, 0:GLA_K] = (gq * jnp.exp(pre_f)).astype(BF16)
    gl_ref[:, GLA_K:2 * GLA_K] = (gk * jnp.exp(-pre_f)).astype(BF16)
    gl_ref[:, 2 * GLA_K:3 * GLA_K] = (gk * jnp.exp(suf_f - lf)).astype(BF16)
    gl_ref[:, 3 * GLA_K:4 * GLA_K] = (gq * jnp.exp(suf_b)).astype(BF16)
    gl_ref[:, 4 * GLA_K:5 * GLA_K] = (gk * jnp.exp(-suf_b)).astype(BF16)
    gl_ref[:, 5 * GLA_K:6 * GLA_K] = (gk * jnp.exp(pre_b - lb)).astype(BF16)
    gl_ref[:, 6 * GLA_K:6 * GLA_K + GLA_V] = gv.astype(BF16)
    dec_ref[:, 0:GLA_K] = jnp.exp(pre_f + suf_f - lf)
    dec_ref[:, GLA_K:2 * GLA_K] = jnp.exp(pre_b + suf_b - lb)
    sr_ref[...] = _silu(gr)

    for j in range(3):
        a, b = j * D_MODEL, (j + 1) * D_MODEL
        sg_ref[:, a:b] = _sigmoid(_dot(u, wa_ref[:, C_GATES + a:C_GATES + b])).astype(BF16)


def _in_proj(x, mods, l, rope_tab, wa, gq, wq, gkv, wk, wvt, wgate, bgate):
    bn, t, d = x.shape
    nt = t // TM
    tok = lambda w: pl.BlockSpec((None, TM, w), lambda b, i: (b, i, 0))
    outs = [(MLA_HEADS * HEAD_PAD, BF16), (MLA_HEADS * HEAD_PAD, BF16), None,
            (CONV_CH, F32), (CONV_CH, F32), (6 * GLA_K + GLA_V, BF16), (2 * GLA_K, F32), (GLA_V, F32),
            (3 * D_MODEL, BF16)]
    vt_rows = MLA_HEADS * HEAD_PAD
    vt_spec = pl.BlockSpec((None, vt_rows, TM), lambda b, i: (b, 0, i))
    vt_shape = jax.ShapeDtypeStruct((bn, vt_rows, t), BF16)
    consts = [wa, gq, wq, gkv, wk, wvt, wgate, bgate]
    return pl.pallas_call(
        _in_proj_kernel,
        grid=(bn, nt),
        in_specs=[tok(d),
                  pl.BlockSpec((None, None, 6, d), lambda b, i: (l, jnp.where(i == 0, bn, b), 0, 0)),
                  pl.BlockSpec((TM, 4 * LANES), lambda b, i: (i, 0))]
                 + [_const_spec(a.shape) for a in consts],
        out_specs=[vt_spec if o is None else tok(o[0]) for o in outs],
        out_shape=[vt_shape if o is None else jax.ShapeDtypeStruct((bn, t, o[0]), o[1]) for o in outs],
        compiler_params=_params("parallel", "parallel"),
        name="in_proj",
    )(x, mods, rope_tab, *consts)


def _attn_kernel(q_ref, k_ref, vt_ref, o_ref, s0_ref, s1_ref, *, ctx_len):
    s_refs = (s0_ref, s1_ref)
    kc = ATTN_KEY_CHUNK

    def scores(j, c, m):
        k = k_ref[c * kc:(c + 1) * kc, j * HEAD_PAD:(j + 1) * HEAD_PAD]
        st = _dot_nt(k, q_ref[:, j * HEAD_PAD:(j + 1) * HEAD_PAD])
        s_refs[j][c * kc:(c + 1) * kc, :] = st
        mc = jnp.max(st, axis=0, keepdims=True)
        return mc if m is None else jnp.maximum(m, mc)

    def weighted(j, c, m, acc):
        p = jnp.exp2(s_refs[j][c * kc:(c + 1) * kc, :] - m).astype(BF16)
        part = _dot(vt_ref[j * HEAD_PAD:(j + 1) * HEAD_PAD, c * kc:(c + 1) * kc], p)
        return part if acc is None else acc + part

    def attend(nk):
        nc = nk // kc
        accs = []
        for j in range(2):
            m, acc = None, None
            for c in range(nc):
                m = scores(j, c, m)
            for c in range(nc):
                acc = weighted(j, c, m, acc)
            accs.append(acc)
        outs = [a[0:MLA_V, :] / a[MLA_V:MLA_V + 1, :] for a in accs]
        o_ref[...] = jnp.concatenate(outs, axis=0).T.astype(o_ref.dtype)

    is_ctx = pl.program_id(2) == 0
    pl.when(is_ctx)(lambda: attend(ctx_len))
    pl.when(jnp.logical_not(is_ctx))(lambda: attend(k_ref.shape[0]))


def _attention(q, k, vt):
    bn, t, _ = q.shape
    hp = MLA_HEADS // 2
    return pl.pallas_call(
        functools.partial(_attn_kernel, ctx_len=TM),
        grid=(bn, hp, t // TM),
        in_specs=[pl.BlockSpec((None, TM, 2 * HEAD_PAD), lambda b, h, i: (b, i, h)),
                  pl.BlockSpec((None, t, 2 * HEAD_PAD), lambda b, h, i: (b, 0, h)),
                  pl.BlockSpec((None, 2 * HEAD_PAD, t), lambda b, h, i: (b, h, 0))],
        out_specs=pl.BlockSpec((None, TM, 2 * MLA_V), lambda b, h, i: (b, i, h)),
        out_shape=jax.ShapeDtypeStruct((bn, t, MLA_HEADS * MLA_V), BF16),
        scratch_shapes=[pltpu.VMEM((t, TM), F32)] * 2,
        compiler_params=_params("parallel", "parallel", "arbitrary"),
        name="attention",
    )(q, k, vt)


def _gla_kernel(qf_ref, vf_ref, df_ref, qb_ref, vb_ref, db_ref, of_ref, ob_ref, sf_ref, sb_ref):
    @pl.when(pl.program_id(1) == 0)
    def _():
        sf_ref[...] = jnp.zeros_like(sf_ref)
        sb_ref[...] = jnp.zeros_like(sb_ref)

    c = GLA_CHUNK
    lane_k = lax.broadcasted_iota(jnp.int32, (c, GLA_K), 1) // GLA_DK
    lane_v = lax.broadcasted_iota(jnp.int32, (c, GLA_V), 1) // GLA_DV
    ai = lax.broadcasted_iota(jnp.int32, (c, GLA_HEADS * c), 0)
    aj = lax.broadcasted_iota(jnp.int32, (c, GLA_HEADS * c), 1) % c
    state_mask = (lax.broadcasted_iota(jnp.int32, (GLA_V, GLA_K), 0) // GLA_DV ==
                  lax.broadcasted_iota(jnp.int32, (GLA_V, GLA_K), 1) // GLA_DK)
    zero = jnp.zeros((), BF16)

    def chunk(g_ref, v_ref, d_ref, o_ref, s_ref, r, causal):
        rows = slice(r * c, (r + 1) * c)
        qd = g_ref[rows, 0:GLA_K]
        kd = g_ref[rows, GLA_K:2 * GLA_K]
        kend = g_ref[rows, 2 * GLA_K:3 * GLA_K]
        v = v_ref[rows, :]
        dec = d_ref[r * c:r * c + 1, :]
        kd_heads = jnp.concatenate([jnp.where(lane_k == h, kd, zero) for h in range(GLA_HEADS)], axis=0)
        v_heads = jnp.concatenate([jnp.where(lane_v == h, v, zero) for h in range(GLA_HEADS)], axis=0)
        att = jnp.where(causal, _dot_nt(qd, kd_heads), 0.0)
        st = s_ref[...]
        o_ref[rows, :] = _dot(att.astype(BF16), v_heads) + _dot_nt(qd, st.astype(BF16))
        s_ref[...] = st * dec + jnp.where(state_mask, _dot_tn(v, kend), 0.0)

    n_chunks = qf_ref.shape[0] // c
    for r in range(n_chunks):
        chunk(qf_ref, vf_ref, df_ref, of_ref, sf_ref, r, aj <= ai)
    for r in reversed(range(n_chunks)):
        chunk(qb_ref, vb_ref, db_ref, ob_ref, sb_ref, r, aj >= ai)


def _gla(gl, dec):
    bn, t, _ = gl.shape
    nt = t // TM
    fwd = lambda i: i
    bwd = lambda i: jnp.where(i == 0, 0, nt - i)
    spec = lambda w, order, col: pl.BlockSpec((None, TM, w), lambda b, i: (b, order(i), col))
    return pl.pallas_call(
        _gla_kernel,
        grid=(bn, nt),
        in_specs=[spec(3 * GLA_K, fwd, 0), spec(GLA_V, fwd, 3), spec(GLA_K, fwd, 0),
                  spec(3 * GLA_K, bwd, 1), spec(GLA_V, bwd, 3), spec(GLA_K, bwd, 1)],
        out_specs=[spec(GLA_V, fwd, 0), spec(GLA_V, bwd, 0)],
        out_shape=[jax.ShapeDtypeStruct((bn, t, GLA_V), F32)] * 2,
        scratch_shapes=[pltpu.VMEM((GLA_V, GLA_K), F32)] * 2,
        compiler_params=_params("parallel", "arbitrary"),
        name="gla",
    )(gl, gl, dec, gl, gl, dec)


def _finish_kernel(x_ref, mod_ref, y_ref, yp_ref, yn_ref, cb_ref, sr_ref, sg_ref, att_ref, of_ref, ob_ref,
                   wmla_ref, cw_ref, cbias_ref, wconv_ref, gnorm_ref, wgla_ref, wout_ref, lng_ref, lnb_ref,
                   o_ref, *, alpha):
    i = pl.program_id(1)
    nt = pl.num_programs(1)
    tm = y_ref.shape[0]
    y = y_ref[...]
    has_prev = jnp.logical_and(i != 0, i != 1)
    has_next = jnp.logical_and(i != 0, i != nt - 1)
    prev_row = jnp.where(has_prev, yp_ref[7:8, :], 0.0)
    next_row = jnp.where(has_next, yn_ref[0:1, :], 0.0)
    row = lax.broadcasted_iota(jnp.int32, y.shape, 0)
    y_m1 = jnp.where(row == 0, prev_row, pltpu.roll(y, 1, axis=0))
    y_p1 = jnp.where(row == tm - 1, next_row, pltpu.roll(y, tm - 1, axis=0))
    conv = cbias_ref[...] + y_m1 * cw_ref[0:1, :] + y * cw_ref[1:2, :] + y_p1 * cw_ref[2:3, :]
    conv = cb_ref[...] * conv

    o = of_ref[...] + ob_ref[...]
    gi = lax.broadcasted_iota(jnp.int32, (GLA_V, GLA_V), 0) // GLA_DV
    gj = lax.broadcasted_iota(jnp.int32, (GLA_V, GLA_V), 1) // GLA_DV
    grp = jnp.where(gi == gj, 1.0 / GLA_DV, 0.0).astype(BF16)
    hi, lo = _split_hi_lo(o * o)
    ms = _dot(hi, grp) + _dot(lo, grp)
    gla = o * lax.rsqrt(ms + NORM_EPS) * gnorm_ref[...] * sr_ref[...]

    d = x_ref.shape[1]
    merged = (sg_ref[:, 0:d].astype(F32) * _dot(att_ref[...], wmla_ref[...])
              + sg_ref[:, d:2 * d].astype(F32) * _dot(conv.astype(BF16), wconv_ref[...])
              + sg_ref[:, 2 * d:3 * d].astype(F32) * _dot(gla.astype(BF16), wgla_ref[...]))
    m = _dot(merged.astype(BF16), wout_ref[...])
    o_ref[...] = _layer_norm(alpha * x_ref[...] + mod_ref[2:3, :] * m, lng_ref[...], lnb_ref[...])


def _finish(x, mods, l, y, cb, sr, sg, att, o_f, o_b, wmla, cw, cbias, wconv, gnorm, wgla, wout, lng, lnb, alpha):
    bn, t, d = x.shape
    nt = t // TM
    tok = lambda w: pl.BlockSpec((None, TM, w), lambda b, i: (b, i, 0))
    rows8 = TM // 8
    consts = [wmla, cw, cbias, wconv, gnorm, wgla, wout, lng, lnb]
    return pl.pallas_call(
        functools.partial(_finish_kernel, alpha=alpha),
        grid=(bn, nt),
        in_specs=[tok(d),
                  pl.BlockSpec((None, None, 6, d), lambda b, i: (l, jnp.where(i == 0, bn, b), 0, 0)),
                  tok(CONV_CH),
                  pl.BlockSpec((None, 8, CONV_CH), lambda b, i: (b, jnp.maximum(i * rows8 - 1, 0), 0)),
                  pl.BlockSpec((None, 8, CONV_CH), lambda b, i: (b, jnp.minimum((i + 1) * rows8, nt * rows8 - 1), 0)),
                  tok(CONV_CH), tok(GLA_V), tok(3 * d), tok(MLA_HEADS * MLA_V), tok(GLA_V), tok(GLA_V)]
                 + [_const_spec(a.shape) for a in consts],
        out_specs=tok(d),
        out_shape=jax.ShapeDtypeStruct((bn, t, d), F32),
        compiler_params=_params("parallel", "parallel"),
        name="finish",
    )(x, mods, y, y, y, cb, sr, sg, att, o_f, o_b, *consts)


def _router_kernel(x_ref, mod_ref, rwh_ref, rwl_ref, rb_ref, u_ref, br_ref, cnt_ref, base_ref):
    first = jnp.logical_and(pl.program_id(0) == 0, pl.program_id(1) == 0)

    @pl.when(first)
    def _():
        base_ref[...] = jnp.zeros_like(base_ref)

    tm = x_ref.shape[0]
    u = x_ref[...] * (1.0 + mod_ref[4:5, :]) + mod_ref[3:4, :]
    hi, lo = _split_hi_lo(u)
    for j in range(TOKEN_ROWS):
        u_ref[pl.ds(j, tm, stride=XS_ROWS), :] = u[:, j * LANES:(j + 1) * LANES]
    logits = _dot(hi, rwh_ref[...]) + _dot(lo, rwh_ref[...]) + _dot(hi, rwl_ref[...])
    s = _sigmoid(logits.T[0:N_EXPERTS, :])
    sel = s + rb_ref[...]

    sel_e = [sel[e:e + 1, :] for e in range(N_EXPERTS)]
    chosen, gscore = [], []
    for g in range(N_GROUPS):
        members = list(range(g * PER_GROUP, (g + 1) * PER_GROUP))
        picked = []
        for a in members:
            beaten = jnp.zeros((1, tm), F32)
            for b in members:
                if b != a:
                    wins = (sel_e[b] >= sel_e[a]) if b < a else (sel_e[b] > sel_e[a])
                    beaten = beaten + jnp.where(wins, 1.0, 0.0)
            picked.append(beaten < 2.0)
        chosen.append(picked)
        score = jnp.zeros((1, tm), F32)
        for a, p in zip(members, picked):
            score = score + jnp.where(p, sel_e[a], 0.0)
        gscore.append(score)
    best = []
    for g in range(N_GROUPS):
        ok = jnp.ones((1, tm), jnp.bool_)
        for h in range(N_GROUPS):
            if h < g:
                ok = jnp.logical_and(ok, gscore[g] > gscore[h])
            elif h > g:
                ok = jnp.logical_and(ok, gscore[g] >= gscore[h])
        best.append(ok)

    zero = jnp.zeros((1, tm), F32)
    w_lo, w_hi = zero, zero
    bucket_rows = []
    for g in range(N_GROUPS):
        taken = [jnp.logical_and(best[g], p) for p in chosen[g]]
        seen = jnp.zeros((1, tm), jnp.bool_)
        for a in range(PER_GROUP):
            sa = s[g * PER_GROUP + a:g * PER_GROUP + a + 1, :]
            w_lo = w_lo + jnp.where(jnp.logical_and(taken[a], jnp.logical_not(seen)), sa, 0.0)
            w_hi = w_hi + jnp.where(jnp.logical_and(taken[a], seen), sa, 0.0)
            seen = jnp.logical_or(seen, taken[a])
        for a in range(PER_GROUP):
            for b in range(a + 1, PER_GROUP):
                bucket_rows.append(jnp.where(jnp.logical_and(taken[a], taken[b]), 1.0, 0.0))
    total = w_lo + w_hi
    w_lo, w_hi = w_lo / total, w_hi / total
    wts = jnp.concatenate([w_lo, w_hi, jnp.zeros((LANES - 2, tm), F32)], axis=0)
    u_ref[pl.ds(TOKEN_ROWS, tm, stride=XS_ROWS), :] = wts.T
    for j in range(TOKEN_ROWS + 1, XS_ROWS):
        u_ref[pl.ds(j, tm, stride=XS_ROWS), :] = jnp.zeros((tm, LANES), F32)

    onehot = jnp.concatenate(bucket_rows + [jnp.zeros((BUCKET_ROWS - N_BUCKETS, tm), F32)], axis=0)
    ri = lax.broadcasted_iota(jnp.int32, (tm, tm), 0)
    ci = lax.broadcasted_iota(jnp.int32, (tm, tm), 1)
    before = _dot(onehot.astype(BF16), jnp.where(ri < ci, 1.0, 0.0).astype(BF16))
    base = base_ref[:, 0:1]
    rank = jnp.sum(onehot * (before + base), axis=0, keepdims=True)
    bidx = lax.broadcasted_iota(jnp.int32, onehot.shape, 0).astype(F32)
    bucket = jnp.sum(onehot * bidx, axis=0, keepdims=True)
    br_ref[...] = jnp.concatenate([bucket, rank, jnp.zeros((6, tm), F32)], axis=0).astype(jnp.int32)
    new_base = base + jnp.sum(onehot, axis=1, keepdims=True)
    base_ref[...] = jnp.broadcast_to(new_base, base_ref.shape)
    cnt_ref[...] = jnp.broadcast_to(new_base, cnt_ref.shape).astype(jnp.int32)


def _router(x1, mods, l, rw_hi, rw_lo, rb):
    bn, t, d = x1.shape
    nt = t // TM
    return pl.pallas_call(
        _router_kernel,
        grid=(bn, nt),
        in_specs=[pl.BlockSpec((None, TM, d), lambda b, i: (b, i, 0)),
                  pl.BlockSpec((None, None, 6, d), lambda b, i: (l, jnp.where(i == 0, bn, b), 0, 0)),
                  _const_spec(rw_hi.shape), _const_spec(rw_lo.shape), _const_spec(rb.shape)],
        out_specs=[pl.BlockSpec((TM * XS_ROWS, LANES), lambda b, i: (b * nt + i, 0)),
                   pl.BlockSpec((None, None, 8, TM), lambda b, i: (b, i, 0, 0)),
                   pl.BlockSpec((BUCKET_ROWS, LANES), lambda b, i: (0, 0))],
        out_shape=[jax.ShapeDtypeStruct((bn * t * XS_ROWS, LANES), F32),
                   jax.ShapeDtypeStruct((bn, nt, 8, TM), jnp.int32),
                   jax.ShapeDtypeStruct((BUCKET_ROWS, LANES), jnp.int32)],
        scratch_shapes=[pltpu.VMEM((BUCKET_ROWS, LANES), F32)],
        compiler_params=_params("arbitrary", "arbitrary"),
        name="router",
    )(x1, mods, rw_hi, rw_lo, rb)


def _place_kernel(off_ref, bucket_ref, rank_ref, pos_ref):
    bucket = bucket_ref[...]
    pos = rank_ref[...]
    for b in range(N_BUCKETS):
        pos = pos + jnp.where(bucket == b, off_ref[b], 0)
    pos_ref[...] = pos


def _place(offsets, bucket, rank):
    return pl.pallas_call(
        _place_kernel,
        in_specs=[pl.BlockSpec(memory_space=pltpu.SMEM), pl.BlockSpec(memory_space=pltpu.VMEM),
                  pl.BlockSpec(memory_space=pltpu.VMEM)],
        out_specs=pl.BlockSpec(memory_space=pltpu.VMEM),
        out_shape=jax.ShapeDtypeStruct(bucket.shape, jnp.int32),
        name="place",
    )(offsets, bucket, rank)


def _slot_copy(src_ref, src_slot, dst_ref, dst_slot, sem, rows):
    return pltpu.make_async_copy(src_ref.at[pl.ds(pl.multiple_of(src_slot * rows, rows), rows)],
                                 dst_ref.at[pl.ds(pl.multiple_of(dst_slot * rows, rows), rows)], sem)


def _invert_kernel(pos_ref, off_ref, cnt_ref, used_ref, src_ref):
    def point_at_zero(p, carry):
        src_ref[p] = 0
        return carry

    for b in range(N_BUCKETS):
        start = off_ref[b] + cnt_ref[b]
        lax.fori_loop(start, (start + (TM - 1)) & (-TM), point_at_zero, 0)
    def unused_tile(i, carry):
        for r in range(TM):
            src_ref[i * TM + r] = 0
        return carry

    lax.fori_loop(used_ref[0], src_ref.shape[0] // TM, unused_tile, 0)

    def one(t, carry):
        src_ref[pos_ref[t]] = t
        return carry

    lax.fori_loop(0, pos_ref.shape[0], one, 0, unroll=ISSUE_UNROLL)


def _invert(pos, offsets, counts, n_used, n_sorted_tiles):
    smem = pl.BlockSpec(memory_space=pltpu.SMEM)
    return pl.pallas_call(
        _invert_kernel,
        in_specs=[smem, smem, smem, smem],
        out_specs=smem,
        out_shape=jax.ShapeDtypeStruct((n_sorted_tiles * TM,), jnp.int32),
        name="invert",
    )(pos.reshape(-1), offsets, counts, n_used).reshape(n_sorted_tiles, TM)


def _expert_kernel(e1_ref, e2_ref, used_ref, src_ref, nxt_ref, u_ref, wg1_ref, wu1_ref, wd1_ref, wg2_ref, wu2_ref,
                   wd2_ref, y_ref, buf0_ref, buf1_ref, sem):
    del e1_ref, e2_ref
    i = pl.program_id(0)
    n_used = used_ref[0]
    tm = buf0_ref.shape[0] // XS_ROWS
    bufs = (buf0_ref, buf1_ref)

    @pl.when(i == 0)
    def _():
        def issue(r, carry):
            _slot_copy(u_ref, src_ref[0, r], buf0_ref, r, sem.at[0], XS_ROWS).start()
            return carry
        lax.fori_loop(0, tm, issue, 0, unroll=ISSUE_UNROLL)

    def run(s):
        cur, nxt = bufs[s], bufs[1 - s]
        pltpu.make_async_copy(u_ref.at[pl.ds(0, tm * XS_ROWS)], cur, sem.at[s]).wait()

        @pl.when(i < n_used)
        def _():
            for r in range(tm):
                _slot_copy(u_ref, nxt_ref[0, r], nxt, r, sem.at[1 - s], XS_ROWS).start()
            x = jnp.concatenate([cur[pl.ds(j, tm, stride=XS_ROWS), :].astype(BF16) for j in range(TOKEN_ROWS)], axis=1)
            wts = cur[pl.ds(TOKEN_ROWS, tm, stride=XS_ROWS), :]
            h1 = _silu(_dot(x, wg1_ref[...])) * _dot(x, wu1_ref[...])
            h2 = _silu(_dot(x, wg2_ref[...])) * _dot(x, wu2_ref[...])
            y = (wts[:, 0:1] * _dot(h1.astype(BF16), wd1_ref[...])
                 + wts[:, 1:2] * _dot(h2.astype(BF16), wd2_ref[...]))
            for j in range(TOKEN_ROWS):
                y_ref[pl.ds(j, tm, stride=TOKEN_ROWS), :] = y[:, j * LANES:(j + 1) * LANES]

    requested = i <= n_used
    pl.when(jnp.logical_and(requested, i % 2 == 0))(lambda: run(0))
    pl.when(jnp.logical_and(requested, i % 2 == 1))(lambda: run(1))

    @pl.when(i >= n_used)
    def _():
        y_ref[...] = jnp.zeros_like(y_ref)


def _experts(tile_e1, tile_e2, n_used, src, u_tok, wg, wu, wd, l):
    n_tiles = src.shape[0]
    d, de = wg.shape[1], wg.shape[2]
    up = lambda tbl: pl.BlockSpec((None, d, de), lambda i, e1, e2, n: (l * N_EXPERTS + tbl(e1, e2)[i], 0, 0))
    down = lambda tbl: pl.BlockSpec((None, de, d), lambda i, e1, e2, n: (l * N_EXPERTS + tbl(e1, e2)[i], 0, 0))
    first = lambda e1, e2: e1
    second = lambda e1, e2: e2
    src3 = src.reshape(n_tiles, 1, TM)
    return pl.pallas_call(
        _expert_kernel,
        grid_spec=pltpu.PrefetchScalarGridSpec(
            num_scalar_prefetch=3,
            grid=(n_tiles,),
            in_specs=[pl.BlockSpec((None, 1, TM), lambda i, e1, e2, n: (i, 0, 0), memory_space=pltpu.SMEM),
                      pl.BlockSpec((None, 1, TM), lambda i, e1, e2, n: (jnp.minimum(i + 1, n_tiles - 1), 0, 0),
                                   memory_space=pltpu.SMEM),
                      pl.BlockSpec(memory_space=pl.ANY),
                      up(first), up(first), down(first), up(second), up(second), down(second)],
            out_specs=pl.BlockSpec((TM * TOKEN_ROWS, LANES), lambda i, e1, e2, n: (i, 0)),
            scratch_shapes=[pltpu.VMEM((TM * XS_ROWS, LANES), F32), pltpu.VMEM((TM * XS_ROWS, LANES), F32),
                            pltpu.SemaphoreType.DMA((2,))],
        ),
        out_shape=jax.ShapeDtypeStruct((n_tiles * TM * TOKEN_ROWS, LANES), F32),
        compiler_params=_params("arbitrary"),
        name="experts",
    )(tile_e1, tile_e2, n_used, src3, src3, u_tok, wg, wu, wd, wg, wu, wd)


def _ffn_out_kernel(pos_ref, nxt_ref, x_ref, mod_ref, ys_ref, lng_ref, lnb_ref, o_ref, buf_ref, sem, *, alpha):
    tm = x_ref.shape[0]
    step = pl.program_id(0) * pl.num_programs(1) + pl.program_id(1)
    n_steps = pl.num_programs(0) * pl.num_programs(1)
    slot = step % 2

    def gather(idx_ref, s):
        def issue(r, carry):
            _slot_copy(ys_ref, idx_ref[0, r], buf_ref.at[s], r, sem.at[s], TOKEN_ROWS).start()
            return carry
        lax.fori_loop(0, tm, issue, 0, unroll=ISSUE_UNROLL)

    pl.when(step == 0)(lambda: gather(pos_ref, 0))
    pl.when(step + 1 < n_steps)(lambda: gather(nxt_ref, 1 - slot))
    cur = buf_ref.at[slot]
    pltpu.make_async_copy(ys_ref.at[pl.ds(0, tm * TOKEN_ROWS)], cur, sem.at[slot]).wait()
    f = jnp.concatenate([cur[pl.ds(j, tm, stride=TOKEN_ROWS), :] for j in range(TOKEN_ROWS)], axis=1)
    o_ref[...] = _layer_norm(alpha * x_ref[...] + mod_ref[5:6, :] * f, lng_ref[...], lnb_ref[...])


def _ffn_out(pos, x1, mods, l, ys, lng, lnb, alpha, first_tile):
    bn, t, d = x1.shape
    nt = t // TM
    ng = nt - first_tile
    pos3 = pos.reshape(bn * nt, 1, TM)

    def next_tile(b, i):
        s = jnp.minimum(b * ng + i + 1, bn * ng - 1)
        return ((s // ng) * nt + s % ng + first_tile, 0, 0)

    return pl.pallas_call(
        functools.partial(_ffn_out_kernel, alpha=alpha),
        grid=(bn, ng),
        in_specs=[pl.BlockSpec((None, 1, TM), lambda b, i: (b * nt + i + first_tile, 0, 0), memory_space=pltpu.SMEM),
                  pl.BlockSpec((None, 1, TM), next_tile, memory_space=pltpu.SMEM),
                  pl.BlockSpec((None, TM, d), lambda b, i: (b, i + first_tile, 0)),
                  pl.BlockSpec((None, None, 6, d), lambda b, i: (l, jnp.where(i + first_tile == 0, bn, b), 0, 0)),
                  pl.BlockSpec(memory_space=pl.ANY),
                  _const_spec(lng.shape), _const_spec(lnb.shape)],
        out_specs=pl.BlockSpec((None, TM, d), lambda b, i: (b, i, 0)),
        out_shape=jax.ShapeDtypeStruct((bn, ng * TM, d), F32),
        scratch_shapes=[pltpu.VMEM((2, TM * TOKEN_ROWS, LANES), F32), pltpu.SemaphoreType.DMA((2,))],
        compiler_params=_params("arbitrary", "arbitrary"),
        name="ffn_out",
    )(pos3, pos3, x1, mods, ys, lng, lnb)


def _rope_tables(ctx_len, seq_len):
    t = np.arange(seq_len)
    half = MLA_ROPE // 2
    inv_freq = 1.0 / (ROPE_BASE ** (np.arange(0, half, 2, dtype=np.float32) / half))
    ang_r = (t // GRID_W).astype(np.float32)[:, None] * inv_freq[None, :]
    ang_c = (t % GRID_W).astype(np.float32)[:, None] * inv_freq[None, :]
    ang = np.concatenate([ang_r, ang_r, ang_c, ang_c], axis=-1).astype(np.float32)
    cos = np.concatenate([np.ones((ctx_len, MLA_ROPE), np.float32), np.cos(ang)], axis=0)
    sin = np.concatenate([np.zeros((ctx_len, MLA_ROPE), np.float32), np.sin(ang)], axis=0)
    n = ctx_len + seq_len
    c_tab = np.zeros((n, LANES), np.float32)
    s_tab = np.zeros((n, LANES), np.float32)
    c_tab[:, :MLA_NOPE] = 1.0
    c_tab[:, MLA_NOPE:QK_DIM] = cos
    s_tab[:, MLA_NOPE:QK_DIM] = sin
    scale = np.float32(QK_DIM ** -0.5 * np.log2(np.e))
    return jnp.asarray(np.concatenate([c_tab * scale, s_tab * scale, c_tab, s_tab], axis=1))


def _rotate_cols(w):
    q = MLA_ROPE // 4
    w1, w2, w3, w4 = (w[..., i * q:(i + 1) * q] for i in range(4))
    return jnp.concatenate([-w2, w1, -w4, w3], axis=-1)


def _prep_layer_weights(w_in, mla_wq_b, mla_wkv_b, gla_w_gate, gla_b_gate):
    depth, d, _ = w_in.shape
    splits = np.cumsum([0, MLA_Q_RANK, MLA_KV_RANK, MLA_ROPE, CONV_CH, CONV_CH, CONV_CH, GLA_K, GLA_K, GLA_V,
                        GLA_V, GLA_GATE_RANK, GLA_GATE_RANK, 3 * D_MODEL])
    seg = [w_in[:, :, splits[i]:splits[i + 1]] for i in range(13)]
    zeros = lambda *s: jnp.zeros(s, F32)
    pad_rope = lambda w: jnp.concatenate([zeros(depth, d, MLA_NOPE), w, zeros(depth, d, LANES - QK_DIM)], axis=-1)
    wa = jnp.concatenate(
        [seg[0], seg[1], pad_rope(seg[2]), pad_rope(_rotate_cols(seg[2])), seg[3], seg[4], seg[5], seg[6], seg[7],
         seg[8], seg[9], seg[10], seg[11], zeros(depth, d, LANES - 2 * GLA_GATE_RANK), seg[12]], axis=-1).astype(BF16)

    r = mla_wq_b.shape[1]
    wq = mla_wq_b.reshape(depth, r, MLA_HEADS, QK_DIM)
    q_main = jnp.concatenate([wq, zeros(depth, r, MLA_HEADS, HEAD_PAD - QK_DIM)], axis=-1)
    q_rot = jnp.concatenate([zeros(depth, r, MLA_HEADS, MLA_NOPE), _rotate_cols(wq[..., MLA_NOPE:]),
                             zeros(depth, r, MLA_HEADS, HEAD_PAD - QK_DIM)], axis=-1)
    wq2 = jnp.concatenate([q_main.reshape(depth, r, -1), q_rot.reshape(depth, r, -1)], axis=-1).astype(BF16)

    rk = mla_wkv_b.shape[1]
    wkv = mla_wkv_b.reshape(depth, rk, MLA_HEADS, MLA_NOPE + MLA_V)
    k_part = jnp.concatenate([wkv[..., :MLA_NOPE], zeros(depth, rk, MLA_HEADS, HEAD_PAD - MLA_NOPE)], axis=-1)
    wk = k_part.reshape(depth, rk, -1).astype(BF16)
    v_part = jnp.concatenate([wkv[..., MLA_NOPE:], zeros(depth, rk, MLA_HEADS, HEAD_PAD - MLA_V)], axis=-1)
    wvt = jnp.swapaxes(v_part.reshape(depth, rk, -1), 1, 2).astype(BF16)

    gr = GLA_GATE_RANK
    wgate = jnp.zeros((depth, LANES, 2 * GLA_K), F32)
    wgate = wgate.at[:, 0:gr, 0:GLA_K].set(gla_w_gate[:, 0]).at[:, gr:2 * gr, GLA_K:].set(gla_w_gate[:, 1])
    bgate = gla_b_gate.reshape(depth, 1, 2 * GLA_K)
    return wa, wq2, wk, wvt, wgate.astype(BF16), bgate


def _bucket_tables(counts, n_sorted_tiles):
    padded = ((counts + TM - 1) // TM) * TM
    ends = jnp.cumsum(padded)
    offsets = ends - padded
    n_used = (ends[-1] // TM).astype(jnp.int32)
    tile_start = jnp.arange(n_sorted_tiles, dtype=jnp.int32) * TM
    tile_bucket = jnp.sum((tile_start[:, None] >= ends[None, :]).astype(jnp.int32), axis=1)
    last_bucket = jnp.sum((jnp.maximum(ends[-1] - TM, 0) >= ends).astype(jnp.int32))
    tile_bucket = jnp.where(tile_start < ends[-1], tile_bucket, last_bucket)
    pair_lo = np.array([a for a in range(PER_GROUP) for b in range(a + 1, PER_GROUP)], np.int32)
    pair_hi = np.array([b for a in range(PER_GROUP) for b in range(a + 1, PER_GROUP)], np.int32)
    group = tile_bucket // N_PAIRS
    pair = tile_bucket % N_PAIRS
    e1 = group * PER_GROUP + jnp.asarray(pair_lo)[pair]
    e2 = group * PER_GROUP + jnp.asarray(pair_hi)[pair]
    return offsets.astype(jnp.int32), e1.astype(jnp.int32), e2.astype(jnp.int32), n_used.reshape(1)


def kernel(x, c, ctx, c_ctx, w_mod, b_mod, w_in, mla_q_norm, mla_wq_b, mla_kv_norm, mla_wkv_b, mla_w_o, conv_w,
           conv_b, conv_w_o, gla_w_gate, gla_b_gate, gla_norm, gla_w_o, w_out, ln1_g, ln1_b, router_w, router_b,
           exp_wg, exp_wu, exp_wd, ln2_g, ln2_b):
    bn, seq_len, d = x.shape
    ctx_len = ctx.shape[1]
    depth = w_in.shape[0]
    assert ctx_len == TM and seq_len % TM == 0 and d == D_MODEL and bn < 8
    alpha = (2 * depth) ** 0.25
    t = ctx_len + seq_len
    n_tok = bn * t
    n_sorted_tiles = n_tok // TM + N_BUCKETS
    n_sorted = n_sorted_tiles * TM

    cond = jnp.concatenate([c, c_ctx[None, :], jnp.zeros((7 - bn, d), F32)], axis=0)
    mods = _modulation(cond, w_mod, b_mod).reshape(depth, 8, 6, d)
    rope_tab = _rope_tables(ctx_len, seq_len)
    wa, wq2, wk, wvt, wgate, bgate = _prep_layer_weights(w_in, mla_wq_b, mla_wkv_b, gla_w_gate, gla_b_gate)
    row = lambda a: a.reshape(depth, 1, a.shape[-1])
    gq, gkv, cbias, gnorm = row(mla_q_norm), row(mla_kv_norm), row(conv_b), row(gla_norm)
    lng1, lnb1, lng2, lnb2 = row(ln1_g), row(ln1_b), row(ln2_g), row(ln2_b)
    wmla, wconv, wgla, wout = (a.astype(BF16) for a in (mla_w_o, conv_w_o, gla_w_o, w_out))
    rw = jnp.concatenate([router_w, jnp.zeros((d, LANES - N_EXPERTS), F32)], axis=1)
    rw_hi = rw.astype(BF16)
    rw_lo = (rw - rw_hi.astype(F32)).astype(BF16)
    rb = router_b.reshape(N_EXPERTS, 1)
    de = exp_wg.shape[-1]
    wg = exp_wg.astype(BF16).reshape(depth * N_EXPERTS, d, de)
    wu = exp_wu.astype(BF16).reshape(depth * N_EXPERTS, d, de)
    wd = exp_wd.astype(BF16).reshape(depth * N_EXPERTS, de, d)

    xt = jnp.concatenate([ctx, x], axis=1)
    for l in range(depth):
        q, k, vt, y, cb, gl, dec, sr, sg = _in_proj(xt, mods, l, rope_tab, wa[l], gq[l], wq2[l], gkv[l], wk[l],
                                                    wvt[l], wgate[l], bgate[l])
        att = _attention(q, k, vt)
        o_f, o_b = _gla(gl, dec)
        x1 = _finish(xt, mods, l, y, cb, sr, sg, att, o_f, o_b, wmla[l], conv_w[l], cbias[l], wconv[l], gnorm[l],
                     wgla[l], wout[l], lng1[l], lnb1[l], alpha)
        u_tok, br, counts = _router(x1, mods, l, rw_hi, rw_lo, rb)
        counts = counts[:N_BUCKETS, 0]
        offsets, tile_e1, tile_e2, n_used = _bucket_tables(counts, n_sorted_tiles)
        br = br.reshape(bn * (t // TM), 8, TM)
        pos = _place(offsets, br[:, 0, :], br[:, 1, :])
        src = _invert(pos, offsets, counts, n_used, n_sorted_tiles)
        ys = _experts(tile_e1, tile_e2, n_used, src, u_tok, wg, wu, wd, l)
        xt = _ffn_out(pos, x1, mods, l, ys, lng2[l], lnb2[l], alpha, first_tile=int(l == depth - 1))
    return xt
```

```python
import functools

import numpy as np
import jax
import jax.numpy as jnp
from jax import lax
from jax.experimental import pallas as pl
from jax.experimental.pallas import tpu as pltpu

D_MODEL = 1024
GRID_W = 64
MLA_HEADS = 8
MLA_NOPE = 64
MLA_ROPE = 32
MLA_V = 64
MLA_Q_RANK = 256
MLA_KV_RANK = 128
ROPE_BASE = 10000.0
CONV_CH = 256
GLA_HEADS = 4
GLA_DK = 32
GLA_DV = 64
GLA_GATE_RANK = 16
GLA_TAU = 16.0
GLA_CHUNK = 64
N_EXPERTS = 16
N_GROUPS = 4
PER_GROUP = N_EXPERTS // N_GROUPS
D_EXPERT = 512
NORM_EPS = 1e-6
F32 = jnp.float32
BF16 = jnp.bfloat16

LANES = 128
TM = 256
HEAD_PAD = 128
QK_DIM = MLA_NOPE + MLA_ROPE
GLA_K = GLA_HEADS * GLA_DK
GLA_V = GLA_HEADS * GLA_DV
N_PAIRS = PER_GROUP * (PER_GROUP - 1) // 2
N_BUCKETS = N_GROUPS * N_PAIRS
BUCKET_ROWS = 32
TOKEN_ROWS = D_MODEL // LANES
XS_ROWS = 2 * TOKEN_ROWS
ISSUE_UNROLL = 8
ATTN_KEY_CHUNK = 256
VMEM_LIMIT = 56 * 1024 * 1024

C_CQ = 0
C_CKV = C_CQ + MLA_Q_RANK
C_KPE = C_CKV + MLA_KV_RANK
C_CONV = C_KPE + 2 * LANES
C_GLA = C_CONV + 3 * CONV_CH
C_GATES = C_GLA + 2 * GLA_K + 2 * GLA_V + LANES
W_A_COLS = C_GATES + 3 * D_MODEL


def _params(*sem):
    return pltpu.CompilerParams(dimension_semantics=sem, vmem_limit_bytes=VMEM_LIMIT)


def _const_spec(shape):
    n = len(shape)
    return pl.BlockSpec(shape, lambda *_: (0,) * n, pipeline_mode=pl.Buffered(1))


def _dot(a, b):
    return jnp.dot(a, b, preferred_element_type=F32)


def _dot_nt(a, b):
    return lax.dot_general(a, b, (((1,), (1,)), ((), ())), preferred_element_type=F32)


def _dot_tn(a, b):
    return lax.dot_general(a, b, (((0,), (0,)), ((), ())), preferred_element_type=F32)


def _split_hi_lo(x):
    hi = x.astype(BF16)
    lo = (x - hi.astype(F32)).astype(BF16)
    return hi, lo


def _sigmoid(x):
    return 1.0 / (1.0 + jnp.exp(-x))


def _silu(x):
    return x * _sigmoid(x)


def _layer_norm(v, g, b):
    mu = jnp.mean(v, axis=-1, keepdims=True)
    d = v - mu
    var = jnp.mean(d * d, axis=-1, keepdims=True)
    return d * lax.rsqrt(var + NORM_EPS) * g + b


def _rms(v, g):
    return v * lax.rsqrt(jnp.mean(v * v, axis=-1, keepdims=True) + NORM_EPS) * g


def _mod_kernel(c_ref, w_ref, b_ref, o_ref):
    o_ref[...] = _dot(_silu(c_ref[...]).astype(BF16), w_ref[...].astype(BF16)) + b_ref[...]


def _modulation(cond, w_mod, b_mod):
    depth, d, n = w_mod.shape
    tn = n // 4
    return pl.pallas_call(
        _mod_kernel,
        grid=(depth, n // tn),
        in_specs=[pl.BlockSpec((8, d), lambda l, j: (0, 0)),
                  pl.BlockSpec((None, d, tn), lambda l, j: (l, 0, j)),
                  pl.BlockSpec((None, 1, tn), lambda l, j: (l, 0, j))],
        out_specs=pl.BlockSpec((None, 8, tn), lambda l, j: (l, 0, j)),
        out_shape=jax.ShapeDtypeStruct((depth, 8, n), F32),
        compiler_params=_params("parallel", "parallel"),
        name="modulation",
    )(cond, w_mod, b_mod.reshape(depth, 1, n))


def _in_proj_kernel(x_ref, mod_ref, rope_ref, wa_ref, gq_ref, wq_ref, gkv_ref, wk_ref, wvt_ref, wgate_ref, bgate_ref,
                    q_ref, k_ref, vt_ref, y_ref, cb_ref, gl_ref, dec_ref, sr_ref, sg_ref):
    u = (x_ref[...] * (1.0 + mod_ref[1:2, :]) + mod_ref[0:1, :]).astype(BF16)
    cq_tab, sq_tab = rope_ref[:, 0:LANES], rope_ref[:, LANES:2 * LANES]
    ck_tab, sk_tab = rope_ref[:, 2 * LANES:3 * LANES], rope_ref[:, 3 * LANES:4 * LANES]

    cq = _dot(u, wa_ref[:, C_CQ:C_CKV])
    q2 = _dot(_rms(cq, gq_ref[...]).astype(BF16), wq_ref[...])
    nq = MLA_HEADS * HEAD_PAD
    for h in range(MLA_HEADS):
        a, b = h * HEAD_PAD, (h + 1) * HEAD_PAD
        q_ref[:, a:b] = (q2[:, a:b] * cq_tab + q2[:, nq + a:nq + b] * sq_tab).astype(BF16)

    ckv = _dot(u, wa_ref[:, C_CKV:C_KPE])
    ckvn = _rms(ckv, gkv_ref[...]).astype(BF16)
    kn = _dot(ckvn, wk_ref[...])
    kpe2 = _dot(u, wa_ref[:, C_KPE:C_CONV])
    kpe = kpe2[:, 0:LANES] * ck_tab + kpe2[:, LANES:2 * LANES] * sk_tab
    for h in range(MLA_HEADS):
        a, b = h * HEAD_PAD, (h + 1) * HEAD_PAD
        k_ref[:, a:b] = (kn[:, a:b] + kpe).astype(BF16)
    vt = _dot_nt(wvt_ref[...], ckvn)
    vrow = lax.broadcasted_iota(jnp.int32, vt.shape, 0) % HEAD_PAD
    vt_ref[...] = jnp.where(vrow == MLA_V, 1.0, vt).astype(BF16)

    cv = _dot(u, wa_ref[:, C_CONV:C_GLA])
    cb_ref[...] = cv[:, 0:CONV_CH]
    y_ref[...] = cv[:, CONV_CH:2 * CONV_CH] * cv[:, 2 * CONV_CH:3 * CONV_CH]

    g = _dot(u, wa_ref[:, C_GLA:C_GATES])
    gq = g[:, 0:GLA_K] * (GLA_DK ** -0.5)
    gk = g[:, GLA_K:2 * GLA_K]
    gv = g[:, 2 * GLA_K:2 * GLA_K + GLA_V]
    gr = g[:, 2 * GLA_K + GLA_V:2 * GLA_K + 2 * GLA_V]
    low = g[:, 2 * GLA_K + 2 * GLA_V:]
    pre = _dot(low.astype(BF16), wgate_ref[...]) + bgate_ref[...]
    logg = (jnp.minimum(pre, 0.0) - jnp.log(1.0 + jnp.exp(-jnp.abs(pre)))) * (1.0 / GLA_TAU)
    lf, lb = logg[:, 0:GLA_K], logg[:, GLA_K:2 * GLA_K]
    tm = lf.shape[0]
    ri = lax.broadcasted_iota(jnp.int32, (tm, tm), 0)
    ci = lax.broadcasted_iota(jnp.int32, (tm, tm), 1)
    same = (ri // GLA_CHUNK) == (ci // GLA_CHUNK)
    m_low = jnp.where(same & (ci <= ri), 1.0, 0.0).astype(BF16)
    m_up = jnp.where(same & (ci >= ri), 1.0, 0.0).astype(BF16)
    pieces = jnp.concatenate(_split_hi_lo(lf) + _split_hi_lo(lb), axis=1)
    pm = _dot(m_low, pieces)
    pu = _dot(m_up, pieces)
    pre_f = pm[:, 0:GLA_K] + pm[:, GLA_K:2 * GLA_K]
    pre_b = pm[:, 2 * GLA_K:3 * GLA_K] + pm[:, 3 * GLA_K:4 * GLA_K]
    suf_f = pu[:, 0:GLA_K] + pu[:, GLA_K:2 * GLA_K]
    suf_b = pu[:, 2 * GLA_K:3 * GLA_K] + pu[:, 3 * GLA_K:4 * GLA_K]
    gl_ref[:, 0:GLA_K] = (gq * jnp.exp(pre_f)).astype(BF16)
    gl_ref[:, GLA_K:2 * GLA_K] = (gk * jnp.exp(-pre_f)).astype(BF16)
    gl_ref[:, 2 * GLA_K:3 * GLA_K] = (gk * jnp.exp(suf_f - lf)).astype(BF16)
    gl_ref[:, 3 * GLA_K:4 * GLA_K] = (gq * jnp.exp(suf_b)).astype(BF16)
    gl_ref[:, 4 * GLA_K:5 * GLA_K] = (gk * jnp.exp(-suf_b)).astype(BF16)
    gl_ref[:, 5 * GLA_K:6 * GLA_K] = (gk * jnp.exp(pre_b - lb)).astype(BF16)
    gl_ref[:, 6 * GLA_K:6 * GLA_K + GLA_V] = gv.astype(BF16)
    dec_ref[:, 0:GLA_K] = jnp.exp(pre_f + suf_f - lf)
    dec_ref[:, GLA_K:2 * GLA_K] = jnp.exp(pre_b + suf_b - lb)
    sr_ref[...] = _silu(gr)

    for j in range(3):
        a, b = j * D_MODEL, (j + 1) * D_MODEL
        sg_ref[:, a:b] = _sigmoid(_dot(u, wa_ref[:, C_GATES + a:C_GATES + b])).astype(BF16)


def _in_proj(x, mods, l, rope_tab, wa, gq, wq, gkv, wk, wvt, wgate, bgate):
    bn, t, d = x.shape
    nt = t // TM
    tok = lambda w: pl.BlockSpec((None, TM, w), lambda b, i: (b, i, 0))
    outs = [(MLA_HEADS * HEAD_PAD, BF16), (MLA_HEADS * HEAD_PAD, BF16), None,
            (CONV_CH, F32), (CONV_CH, F32), (6 * GLA_K + GLA_V, BF16), (2 * GLA_K, F32), (GLA_V, F32),
            (3 * D_MODEL, BF16)]
    vt_rows = MLA_HEADS * HEAD_PAD
    vt_spec = pl.BlockSpec((None, vt_rows, TM), lambda b, i: (b, 0, i))
    vt_shape = jax.ShapeDtypeStruct((bn, vt_rows, t), BF16)
    consts = [wa, gq, wq, gkv, wk, wvt, wgate, bgate]
    return pl.pallas_call(
        _in_proj_kernel,
        grid=(bn, nt),
        in_specs=[tok(d),
                  pl.BlockSpec((None, None, 6, d), lambda b, i: (l, jnp.where(i == 0, bn, b), 0, 0)),
                  pl.BlockSpec((TM, 4 * LANES), lambda b, i: (i, 0))]
                 + [_const_spec(a.shape) for a in consts],
        out_specs=[vt_spec if o is None else tok(o[0]) for o in outs],
        out_shape=[vt_shape if o is None else jax.ShapeDtypeStruct((bn, t, o[0]), o[1]) for o in outs],
        compiler_params=_params("parallel", "parallel"),
        name="in_proj",
    )(x, mods, rope_tab, *consts)


def _attn_kernel(q_ref, k_ref, vt_ref, o_ref, s0_ref, s1_ref, *, ctx_len):
    s_refs = (s0_ref, s1_ref)
    kc = ATTN_KEY_CHUNK

    def scores(j, c, m):
        k = k_ref[c * kc:(c + 1) * kc, j * HEAD_PAD:(j + 1) * HEAD_PAD]
        st = _dot_nt(k, q_ref[:, j * HEAD_PAD:(j + 1) * HEAD_PAD])
        s_refs[j][c * kc:(c + 1) * kc, :] = st
        mc = jnp.max(st, axis=0, keepdims=True)
        return mc if m is None else jnp.maximum(m, mc)

    def weighted(j, c, m, acc):
        p = jnp.exp2(s_refs[j][c * kc:(c + 1) * kc, :] - m).astype(BF16)
        part = _dot(vt_ref[j * HEAD_PAD:(j + 1) * HEAD_PAD, c * kc:(c + 1) * kc], p)
        return part if acc is None else acc + part

    def attend(nk):
        nc = nk // kc
        accs = []
        for j in range(2):
            m, acc = None, None
            for c in range(nc):
                m = scores(j, c, m)
            for c in range(nc):
                acc = weighted(j, c, m, acc)
            accs.append(acc)
        outs = [a[0:MLA_V, :] / a[MLA_V:MLA_V + 1, :] for a in accs]
        o_ref[...] = jnp.concatenate(outs, axis=0).T.astype(o_ref.dtype)

    is_ctx = pl.program_id(2) == 0
    pl.when(is_ctx)(lambda: attend(ctx_len))
    pl.when(jnp.logical_not(is_ctx))(lambda: attend(k_ref.shape[0]))


def _attention(q, k, vt):
    bn, t, _ = q.shape
    hp = MLA_HEADS // 2
    return pl.pallas_call(
        functools.partial(_attn_kernel, ctx_len=TM),
        grid=(bn, hp, t // TM),
        in_specs=[pl.BlockSpec((None, TM, 2 * HEAD_PAD), lambda b, h, i: (b, i, h)),
                  pl.BlockSpec((None, t, 2 * HEAD_PAD), lambda b, h, i: (b, 0, h)),
                  pl.BlockSpec((None, 2 * HEAD_PAD, t), lambda b, h, i: (b, h, 0))],
        out_specs=pl.BlockSpec((None, TM, 2 * MLA_V), lambda b, h, i: (b, i, h)),
        out_shape=jax.ShapeDtypeStruct((bn, t, MLA_HEADS * MLA_V), BF16),
        scratch_shapes=[pltpu.VMEM((t, TM), F32)] * 2,
        compiler_params=_params("parallel", "parallel", "arbitrary"),
        name="attention",
    )(q, k, vt)


def _gla_kernel(qf_ref, vf_ref, df_ref, qb_ref, vb_ref, db_ref, of_ref, ob_ref, sf_ref, sb_ref):
    @pl.when(pl.program_id(1) == 0)
    def _():
        sf_ref[...] = jnp.zeros_like(sf_ref)
        sb_ref[...] = jnp.zeros_like(sb_ref)

    c = GLA_CHUNK
    lane_k = lax.broadcasted_iota(jnp.int32, (c, GLA_K), 1) // GLA_DK
    lane_v = lax.broadcasted_iota(jnp.int32, (c, GLA_V), 1) // GLA_DV
    ai = lax.broadcasted_iota(jnp.int32, (c, GLA_HEADS * c), 0)
    aj = lax.broadcasted_iota(jnp.int32, (c, GLA_HEADS * c), 1) % c
    state_mask = (lax.broadcasted_iota(jnp.int32, (GLA_V, GLA_K), 0) // GLA_DV ==
                  lax.broadcasted_iota(jnp.int32, (GLA_V, GLA_K), 1) // GLA_DK)
    zero = jnp.zeros((), BF16)

    def chunk(g_ref, v_ref, d_ref, o_ref, s_ref, b, r, causal):
        rows = slice(r * c, (r + 1) * c)
        qd = g_ref[b, rows, 0:GLA_K]
        kd = g_ref[b, rows, GLA_K:2 * GLA_K]
        kend = g_ref[b, rows, 2 * GLA_K:3 * GLA_K]
        v = v_ref[b, rows, :]
        dec = d_ref[b, r * c:r * c + 1, :]
        kd_heads = jnp.concatenate([jnp.where(lane_k == h, kd, zero) for h in range(GLA_HEADS)], axis=0)
        v_heads = jnp.concatenate([jnp.where(lane_v == h, v, zero) for h in range(GLA_HEADS)], axis=0)
        att = jnp.where(causal, _dot_nt(qd, kd_heads), 0.0)
        st = s_ref[b]
        o_ref[b, rows, :] = _dot(att.astype(BF16), v_heads) + _dot_nt(qd, st.astype(BF16))
        s_ref[b] = st * dec + jnp.where(state_mask, _dot_tn(v, kend), 0.0)

    n_chunks = qf_ref.shape[1] // c
    for r in range(n_chunks):
        for b in range(qf_ref.shape[0]):
            chunk(qf_ref, vf_ref, df_ref, of_ref, sf_ref, b, r, aj <= ai)
            chunk(qb_ref, vb_ref, db_ref, ob_ref, sb_ref, b, n_chunks - 1 - r, aj >= ai)


def _gla(gl, dec):
    bn, t, _ = gl.shape
    nt = t // TM
    gb = 2 if bn % 2 == 0 else 1
    fwd = lambda i: i
    bwd = lambda i: jnp.where(i == 0, 0, nt - i)
    spec = lambda w, order, col: pl.BlockSpec((gb, TM, w), lambda b, i: (b, order(i), col))
    return pl.pallas_call(
        _gla_kernel,
        grid=(bn // gb, nt),
        in_specs=[spec(3 * GLA_K, fwd, 0), spec(GLA_V, fwd, 3), spec(GLA_K, fwd, 0),
                  spec(3 * GLA_K, bwd, 1), spec(GLA_V, bwd, 3), spec(GLA_K, bwd, 1)],
        out_specs=[spec(GLA_V, fwd, 0), spec(GLA_V, bwd, 0)],
        out_shape=[jax.ShapeDtypeStruct((bn, t, GLA_V), F32)] * 2,
        scratch_shapes=[pltpu.VMEM((gb, GLA_V, GLA_K), F32)] * 2,
        compiler_params=_params("parallel", "arbitrary"),
        name="gla",
    )(gl, gl, dec, gl, gl, dec)


def _finish_kernel(x_ref, mod_ref, y_ref, yp_ref, yn_ref, cb_ref, sr_ref, sg_ref, att_ref, of_ref, ob_ref,
                   wmla_ref, cw_ref, cbias_ref, wconv_ref, gnorm_ref, wgla_ref, wout_ref, lng_ref, lnb_ref,
                   rwh_ref, rwl_ref, rb_ref, o_ref, u_ref, br_ref, cnt_ref, base_ref, *, alpha):
    i = pl.program_id(1)
    nt = pl.num_programs(1)
    tm = y_ref.shape[0]
    y = y_ref[...]
    has_prev = jnp.logical_and(i != 0, i != 1)
    has_next = jnp.logical_and(i != 0, i != nt - 1)
    prev_row = jnp.where(has_prev, yp_ref[7:8, :], 0.0)
    next_row = jnp.where(has_next, yn_ref[0:1, :], 0.0)
    row = lax.broadcasted_iota(jnp.int32, y.shape, 0)
    y_m1 = jnp.where(row == 0, prev_row, pltpu.roll(y, 1, axis=0))
    y_p1 = jnp.where(row == tm - 1, next_row, pltpu.roll(y, tm - 1, axis=0))
    conv = cbias_ref[...] + y_m1 * cw_ref[0:1, :] + y * cw_ref[1:2, :] + y_p1 * cw_ref[2:3, :]
    conv = cb_ref[...] * conv

    o = of_ref[...] + ob_ref[...]
    gi = lax.broadcasted_iota(jnp.int32, (GLA_V, GLA_V), 0) // GLA_DV
    gj = lax.broadcasted_iota(jnp.int32, (GLA_V, GLA_V), 1) // GLA_DV
    grp = jnp.where(gi == gj, 1.0 / GLA_DV, 0.0).astype(BF16)
    hi, lo = _split_hi_lo(o * o)
    ms = _dot(hi, grp) + _dot(lo, grp)
    gla = o * lax.rsqrt(ms + NORM_EPS) * gnorm_ref[...] * sr_ref[...]

    d = x_ref.shape[1]
    merged = (sg_ref[:, 0:d].astype(F32) * _dot(att_ref[...], wmla_ref[...])
              + sg_ref[:, d:2 * d].astype(F32) * _dot(conv.astype(BF16), wconv_ref[...])
              + sg_ref[:, 2 * d:3 * d].astype(F32) * _dot(gla.astype(BF16), wgla_ref[...]))
    m = _dot(merged.astype(BF16), wout_ref[...])
    x1 = _layer_norm(alpha * x_ref[...] + mod_ref[2:3, :] * m, lng_ref[...], lnb_ref[...])
    o_ref[...] = x1
    _route_tile(x1, mod_ref, rwh_ref, rwl_ref, rb_ref, u_ref, br_ref, cnt_ref, base_ref)


def _finish(x, mods, l, y, cb, sr, sg, att, o_f, o_b, wmla, cw, cbias, wconv, gnorm, wgla, wout, lng, lnb, rw_hi,
            rw_lo, rb, alpha):
    bn, t, d = x.shape
    nt = t // TM
    tok = lambda w: pl.BlockSpec((None, TM, w), lambda b, i: (b, i, 0))
    rows8 = TM // 8
    consts = [wmla, cw, cbias, wconv, gnorm, wgla, wout, lng, lnb, rw_hi, rw_lo, rb]
    return pl.pallas_call(
        functools.partial(_finish_kernel, alpha=alpha),
        grid=(bn, nt),
        in_specs=[tok(d),
                  pl.BlockSpec((None, None, 6, d), lambda b, i: (l, jnp.where(i == 0, bn, b), 0, 0)),
                  tok(CONV_CH),
                  pl.BlockSpec((None, 8, CONV_CH), lambda b, i: (b, jnp.maximum(i * rows8 - 1, 0), 0)),
                  pl.BlockSpec((None, 8, CONV_CH), lambda b, i: (b, jnp.minimum((i + 1) * rows8, nt * rows8 - 1), 0)),
                  tok(CONV_CH), tok(GLA_V), tok(3 * d), tok(MLA_HEADS * MLA_V), tok(GLA_V), tok(GLA_V)]
                 + [_const_spec(a.shape) for a in consts],
        out_specs=[tok(d),
                   pl.BlockSpec((TM * XS_ROWS, LANES), lambda b, i: (b * nt + i, 0)),
                   pl.BlockSpec((None, None, 8, TM), lambda b, i: (b, i, 0, 0)),
                   pl.BlockSpec((BUCKET_ROWS, LANES), lambda b, i: (0, 0))],
        out_shape=[jax.ShapeDtypeStruct((bn, t, d), F32),
                   jax.ShapeDtypeStruct((bn * t * XS_ROWS, LANES), F32),
                   jax.ShapeDtypeStruct((bn, nt, 8, TM), jnp.int32),
                   jax.ShapeDtypeStruct((BUCKET_ROWS, LANES), jnp.int32)],
        scratch_shapes=[pltpu.VMEM((BUCKET_ROWS, LANES), F32)],
        compiler_params=_params("arbitrary", "arbitrary"),
        name="finish",
    )(x, mods, y, y, y, cb, sr, sg, att, o_f, o_b, *consts)


def _route_tile(x1, mod_ref, rwh_ref, rwl_ref, rb_ref, u_ref, br_ref, cnt_ref, base_ref):
    first = jnp.logical_and(pl.program_id(0) == 0, pl.program_id(1) == 0)

    @pl.when(first)
    def _():
        base_ref[...] = jnp.zeros_like(base_ref)

    tm = x1.shape[0]
    u = x1 * (1.0 + mod_ref[4:5, :]) + mod_ref[3:4, :]
    hi, lo = _split_hi_lo(u)
    for j in range(TOKEN_ROWS):
        u_ref[pl.ds(j, tm, stride=XS_ROWS), :] = u[:, j * LANES:(j + 1) * LANES]
    logits = _dot(hi, rwh_ref[...]) + _dot(lo, rwh_ref[...]) + _dot(hi, rwl_ref[...])
    s = _sigmoid(logits.T[0:N_EXPERTS, :])
    sel = s + rb_ref[...]

    sel_e = [sel[e:e + 1, :] for e in range(N_EXPERTS)]
    chosen, gscore = [], []
    for g in range(N_GROUPS):
        members = list(range(g * PER_GROUP, (g + 1) * PER_GROUP))
        picked = []
        for a in members:
            beaten = jnp.zeros((1, tm), F32)
            for b in members:
                if b != a:
                    wins = (sel_e[b] >= sel_e[a]) if b < a else (sel_e[b] > sel_e[a])
                    beaten = beaten + jnp.where(wins, 1.0, 0.0)
            picked.append(beaten < 2.0)
        chosen.append(picked)
        score = jnp.zeros((1, tm), F32)
        for a, p in zip(members, picked):
            score = score + jnp.where(p, sel_e[a], 0.0)
        gscore.append(score)
    best = []
    for g in range(N_GROUPS):
        ok = jnp.ones((1, tm), jnp.bool_)
        for h in range(N_GROUPS):
            if h < g:
                ok = jnp.logical_and(ok, gscore[g] > gscore[h])
            elif h > g:
                ok = jnp.logical_and(ok, gscore[g] >= gscore[h])
        best.append(ok)

    zero = jnp.zeros((1, tm), F32)
    w_lo, w_hi = zero, zero
    bucket_rows = []
    for g in range(N_GROUPS):
        taken = [jnp.logical_and(best[g], p) for p in chosen[g]]
        seen = jnp.zeros((1, tm), jnp.bool_)
        for a in range(PER_GROUP):
            sa = s[g * PER_GROUP + a:g * PER_GROUP + a + 1, :]
            w_lo = w_lo + jnp.where(jnp.logical_and(taken[a], jnp.logical_not(seen)), sa, 0.0)
            w_hi = w_hi + jnp.where(jnp.logical_and(taken[a], seen), sa, 0.0)
            seen = jnp.logical_or(seen, taken[a])
        for a in range(PER_GROUP):
            for b in range(a + 1, PER_GROUP):
                bucket_rows.append(jnp.where(jnp.logical_and(taken[a], taken[b]), 1.0, 0.0))
    total = w_lo + w_hi
    w_lo, w_hi = w_lo / total, w_hi / total
    wts = jnp.concatenate([w_lo, w_hi, jnp.zeros((LANES - 2, tm), F32)], axis=0)
    u_ref[pl.ds(TOKEN_ROWS, tm, stride=XS_ROWS), :] = wts.T
    for j in range(TOKEN_ROWS + 1, XS_ROWS):
        u_ref[pl.ds(j, tm, stride=XS_ROWS), :] = jnp.zeros((tm, LANES), F32)

    onehot = jnp.concatenate(bucket_rows + [jnp.zeros((BUCKET_ROWS - N_BUCKETS, tm), F32)], axis=0)
    ri = lax.broadcasted_iota(jnp.int32, (tm, tm), 0)
    ci = lax.broadcasted_iota(jnp.int32, (tm, tm), 1)
    before = _dot(onehot.astype(BF16), jnp.where(ri < ci, 1.0, 0.0).astype(BF16))
    base = base_ref[:, 0:1]
    rank = jnp.sum(onehot * (before + base), axis=0, keepdims=True)
    bidx = lax.broadcasted_iota(jnp.int32, onehot.shape, 0).astype(F32)
    bucket = jnp.sum(onehot * bidx, axis=0, keepdims=True)
    br_ref[...] = jnp.concatenate([bucket, rank, jnp.zeros((6, tm), F32)], axis=0).astype(jnp.int32)
    new_base = base + jnp.sum(onehot, axis=1, keepdims=True)
    base_ref[...] = jnp.broadcast_to(new_base, base_ref.shape)
    cnt_ref[...] = jnp.broadcast_to(new_base, cnt_ref.shape).astype(jnp.int32)


def _place_kernel(off_ref, bucket_ref, rank_ref, pos_ref):
    bucket = bucket_ref[...]
    pos = rank_ref[...]
    for b in range(N_BUCKETS):
        pos = pos + jnp.where(bucket == b, off_ref[b], 0)
    pos_ref[...] = pos


def _place(offsets, bucket, rank):
    return pl.pallas_call(
        _place_kernel,
        in_specs=[pl.BlockSpec(memory_space=pltpu.SMEM), pl.BlockSpec(memory_space=pltpu.VMEM),
                  pl.BlockSpec(memory_space=pltpu.VMEM)],
        out_specs=pl.BlockSpec(memory_space=pltpu.VMEM),
        out_shape=jax.ShapeDtypeStruct(bucket.shape, jnp.int32),
        name="place",
    )(offsets, bucket, rank)


def _slot_copy(src_ref, src_slot, dst_ref, dst_slot, sem, rows):
    return pltpu.make_async_copy(src_ref.at[pl.ds(pl.multiple_of(src_slot * rows, rows), rows)],
                                 dst_ref.at[pl.ds(pl.multiple_of(dst_slot * rows, rows), rows)], sem)


def _invert_kernel(pos_ref, off_ref, cnt_ref, used_ref, src_ref):
    def point_at_zero(p, carry):
        src_ref[p] = 0
        return carry

    for b in range(N_BUCKETS):
        start = off_ref[b] + cnt_ref[b]
        lax.fori_loop(start, (start + (TM - 1)) & (-TM), point_at_zero, 0)
    def unused_tile(i, carry):
        for r in range(TM):
            src_ref[i * TM + r] = 0
        return carry

    lax.fori_loop(used_ref[0], src_ref.shape[0] // TM, unused_tile, 0)

    def one(t, carry):
        src_ref[pos_ref[t]] = t
        return carry

    lax.fori_loop(0, pos_ref.shape[0], one, 0, unroll=ISSUE_UNROLL)


def _invert(pos, offsets, counts, n_used, n_sorted_tiles):
    smem = pl.BlockSpec(memory_space=pltpu.SMEM)
    return pl.pallas_call(
        _invert_kernel,
        in_specs=[smem, smem, smem, smem],
        out_specs=smem,
        out_shape=jax.ShapeDtypeStruct((n_sorted_tiles * TM,), jnp.int32),
        name="invert",
    )(pos.reshape(-1), offsets, counts, n_used).reshape(n_sorted_tiles, TM)


def _expert_kernel(e1_ref, e2_ref, used_ref, src_ref, nxt_ref, u_ref, wg1_ref, wu1_ref, wd1_ref, wg2_ref, wu2_ref,
                   wd2_ref, y_ref, buf0_ref, buf1_ref, sem):
    del e1_ref, e2_ref
    i = pl.program_id(0)
    n_used = used_ref[0]
    tm = buf0_ref.shape[0] // XS_ROWS
    bufs = (buf0_ref, buf1_ref)

    @pl.when(i == 0)
    def _():
        def issue(r, carry):
            _slot_copy(u_ref, src_ref[0, r], buf0_ref, r, sem.at[0], XS_ROWS).start()
            return carry
        lax.fori_loop(0, tm, issue, 0, unroll=ISSUE_UNROLL)

    def run(s):
        cur, nxt = bufs[s], bufs[1 - s]

        @pl.when(i < n_used)
        def _():
            for r in range(tm):
                _slot_copy(u_ref, nxt_ref[0, r], nxt, r, sem.at[1 - s], XS_ROWS).start()

        pltpu.make_async_copy(u_ref.at[pl.ds(0, tm * XS_ROWS)], cur, sem.at[s]).wait()

        @pl.when(i < n_used)
        def _():
            x = jnp.concatenate([cur[pl.ds(j, tm, stride=XS_ROWS), :].astype(BF16) for j in range(TOKEN_ROWS)], axis=1)
            wts = cur[pl.ds(TOKEN_ROWS, tm, stride=XS_ROWS), :]
            h1 = _silu(_dot(x, wg1_ref[...])) * _dot(x, wu1_ref[...])
            h2 = _silu(_dot(x, wg2_ref[...])) * _dot(x, wu2_ref[...])
            y = (wts[:, 0:1] * _dot(h1.astype(BF16), wd1_ref[...])
                 + wts[:, 1:2] * _dot(h2.astype(BF16), wd2_ref[...]))
            for j in range(TOKEN_ROWS):
                y_ref[pl.ds(j, tm, stride=TOKEN_ROWS), :] = y[:, j * LANES:(j + 1) * LANES]

    requested = i <= n_used
    pl.when(jnp.logical_and(requested, i % 2 == 0))(lambda: run(0))
    pl.when(jnp.logical_and(requested, i % 2 == 1))(lambda: run(1))

    @pl.when(i >= n_used)
    def _():
        y_ref[...] = jnp.zeros_like(y_ref)


def _experts(tile_e1, tile_e2, n_used, src, u_tok, wg, wu, wd, l):
    n_tiles = src.shape[0]
    d, de = wg.shape[1], wg.shape[2]
    up = lambda tbl: pl.BlockSpec((None, d, de), lambda i, e1, e2, n: (l * N_EXPERTS + tbl(e1, e2)[i], 0, 0))
    down = lambda tbl: pl.BlockSpec((None, de, d), lambda i, e1, e2, n: (l * N_EXPERTS + tbl(e1, e2)[i], 0, 0))
    first = lambda e1, e2: e1
    second = lambda e1, e2: e2
    src3 = src.reshape(n_tiles, 1, TM)
    return pl.pallas_call(
        _expert_kernel,
        grid_spec=pltpu.PrefetchScalarGridSpec(
            num_scalar_prefetch=3,
            grid=(n_tiles,),
            in_specs=[pl.BlockSpec((None, 1, TM), lambda i, e1, e2, n: (i, 0, 0), memory_space=pltpu.SMEM),
                      pl.BlockSpec((None, 1, TM), lambda i, e1, e2, n: (jnp.minimum(i + 1, n_tiles - 1), 0, 0),
                                   memory_space=pltpu.SMEM),
                      pl.BlockSpec(memory_space=pl.ANY),
                      up(first), up(first), down(first), up(second), up(second), down(second)],
            out_specs=pl.BlockSpec((TM * TOKEN_ROWS, LANES), lambda i, e1, e2, n: (i, 0)),
            scratch_shapes=[pltpu.VMEM((TM * XS_ROWS, LANES), F32), pltpu.VMEM((TM * XS_ROWS, LANES), F32),
                            pltpu.SemaphoreType.DMA((2,))],
        ),
        out_shape=jax.ShapeDtypeStruct((n_tiles * TM * TOKEN_ROWS, LANES), F32),
        compiler_params=_params("arbitrary"),
        name="experts",
    )(tile_e1, tile_e2, n_used, src3, src3, u_tok, wg, wu, wd, wg, wu, wd)


def _ffn_out_kernel(pos_ref, nxt_ref, x_ref, mod_ref, ys_ref, lng_ref, lnb_ref, o_ref, buf_ref, sem, *, alpha):
    tm = x_ref.shape[0]
    step = pl.program_id(0) * pl.num_programs(1) + pl.program_id(1)
    n_steps = pl.num_programs(0) * pl.num_programs(1)
    slot = step % 2

    @pl.when(step == 0)
    def _():
        def issue(r, carry):
            _slot_copy(ys_ref, pos_ref[0, r], buf_ref.at[0], r, sem.at[0], TOKEN_ROWS).start()
            return carry
        lax.fori_loop(0, tm, issue, 0, unroll=ISSUE_UNROLL)

    @pl.when(step + 1 < n_steps)
    def _():
        for r in range(tm):
            _slot_copy(ys_ref, nxt_ref[0, r], buf_ref.at[1 - slot], r, sem.at[1 - slot], TOKEN_ROWS).start()
    cur = buf_ref.at[slot]
    pltpu.make_async_copy(ys_ref.at[pl.ds(0, tm * TOKEN_ROWS)], cur, sem.at[slot]).wait()
    f = jnp.concatenate([cur[pl.ds(j, tm, stride=TOKEN_ROWS), :] for j in range(TOKEN_ROWS)], axis=1)
    o_ref[...] = _layer_norm(alpha * x_ref[...] + mod_ref[5:6, :] * f, lng_ref[...], lnb_ref[...])


def _ffn_out(pos, x1, mods, l, ys, lng, lnb, alpha, first_tile):
    bn, t, d = x1.shape
    nt = t // TM
    ng = nt - first_tile
    pos3 = pos.reshape(bn * nt, 1, TM)

    def next_tile(b, i):
        s = jnp.minimum(b * ng + i + 1, bn * ng - 1)
        return ((s // ng) * nt + s % ng + first_tile, 0, 0)

    return pl.pallas_call(
        functools.partial(_ffn_out_kernel, alpha=alpha),
        grid=(bn, ng),
        in_specs=[pl.BlockSpec((None, 1, TM), lambda b, i: (b * nt + i + first_tile, 0, 0), memory_space=pltpu.SMEM),
                  pl.BlockSpec((None, 1, TM), next_tile, memory_space=pltpu.SMEM),
                  pl.BlockSpec((None, TM, d), lambda b, i: (b, i + first_tile, 0)),
                  pl.BlockSpec((None, None, 6, d), lambda b, i: (l, jnp.where(i + first_tile == 0, bn, b), 0, 0)),
                  pl.BlockSpec(memory_space=pl.ANY),
                  _const_spec(lng.shape), _const_spec(lnb.shape)],
        out_specs=pl.BlockSpec((None, TM, d), lambda b, i: (b, i, 0)),
        out_shape=jax.ShapeDtypeStruct((bn, ng * TM, d), F32),
        scratch_shapes=[pltpu.VMEM((2, TM * TOKEN_ROWS, LANES), F32), pltpu.SemaphoreType.DMA((2,))],
        compiler_params=_params("arbitrary", "arbitrary"),
        name="ffn_out",
    )(pos3, pos3, x1, mods, ys, lng, lnb)


def _rope_tables(ctx_len, seq_len):
    t = np.arange(seq_len)
    half = MLA_ROPE // 2
    inv_freq = 1.0 / (ROPE_BASE ** (np.arange(0, half, 2, dtype=np.float32) / half))
    ang_r = (t // GRID_W).astype(np.float32)[:, None] * inv_freq[None, :]
    ang_c = (t % GRID_W).astype(np.float32)[:, None] * inv_freq[None, :]
    ang = np.concatenate([ang_r, ang_r, ang_c, ang_c], axis=-1).astype(np.float32)
    cos = np.concatenate([np.ones((ctx_len, MLA_ROPE), np.float32), np.cos(ang)], axis=0)
    sin = np.concatenate([np.zeros((ctx_len, MLA_ROPE), np.float32), np.sin(ang)], axis=0)
    n = ctx_len + seq_len
    c_tab = np.zeros((n, LANES), np.float32)
    s_tab = np.zeros((n, LANES), np.float32)
    c_tab[:, :MLA_NOPE] = 1.0
    c_tab[:, MLA_NOPE:QK_DIM] = cos
    s_tab[:, MLA_NOPE:QK_DIM] = sin
    scale = np.float32(QK_DIM ** -0.5 * np.log2(np.e))
    return jnp.asarray(np.concatenate([c_tab * scale, s_tab * scale, c_tab, s_tab], axis=1))


def _rotate_cols(w):
    q = MLA_ROPE // 4
    w1, w2, w3, w4 = (w[..., i * q:(i + 1) * q] for i in range(4))
    return jnp.concatenate([-w2, w1, -w4, w3], axis=-1)


def _prep_layer_weights(w_in, mla_wq_b, mla_wkv_b, gla_w_gate, gla_b_gate):
    depth, d, _ = w_in.shape
    splits = np.cumsum([0, MLA_Q_RANK, MLA_KV_RANK, MLA_ROPE, CONV_CH, CONV_CH, CONV_CH, GLA_K, GLA_K, GLA_V,
                        GLA_V, GLA_GATE_RANK, GLA_GATE_RANK, 3 * D_MODEL])
    seg = [w_in[:, :, splits[i]:splits[i + 1]] for i in range(13)]
    zeros = lambda *s: jnp.zeros(s, F32)
    pad_rope = lambda w: jnp.concatenate([zeros(depth, d, MLA_NOPE), w, zeros(depth, d, LANES - QK_DIM)], axis=-1)
    wa = jnp.concatenate(
        [seg[0], seg[1], pad_rope(seg[2]), pad_rope(_rotate_cols(seg[2])), seg[3], seg[4], seg[5], seg[6], seg[7],
         seg[8], seg[9], seg[10], seg[11], zeros(depth, d, LANES - 2 * GLA_GATE_RANK), seg[12]], axis=-1).astype(BF16)

    r = mla_wq_b.shape[1]
    wq = mla_wq_b.reshape(depth, r, MLA_HEADS, QK_DIM)
    q_main = jnp.concatenate([wq, zeros(depth, r, MLA_HEADS, HEAD_PAD - QK_DIM)], axis=-1)
    q_rot = jnp.concatenate([zeros(depth, r, MLA_HEADS, MLA_NOPE), _rotate_cols(wq[..., MLA_NOPE:]),
                             zeros(depth, r, MLA_HEADS, HEAD_PAD - QK_DIM)], axis=-1)
    wq2 = jnp.concatenate([q_main.reshape(depth, r, -1), q_rot.reshape(depth, r, -1)], axis=-1).astype(BF16)

    rk = mla_wkv_b.shape[1]
    wkv = mla_wkv_b.reshape(depth, rk, MLA_HEADS, MLA_NOPE + MLA_V)
    k_part = jnp.concatenate([wkv[..., :MLA_NOPE], zeros(depth, rk, MLA_HEADS, HEAD_PAD - MLA_NOPE)], axis=-1)
    wk = k_part.reshape(depth, rk, -1).astype(BF16)
    v_part = jnp.concatenate([wkv[..., MLA_NOPE:], zeros(depth, rk, MLA_HEADS, HEAD_PAD - MLA_V)], axis=-1)
    wvt = jnp.swapaxes(v_part.reshape(depth, rk, -1), 1, 2).astype(BF16)

    gr = GLA_GATE_RANK
    wgate = jnp.zeros((depth, LANES, 2 * GLA_K), F32)
    wgate = wgate.at[:, 0:gr, 0:GLA_K].set(gla_w_gate[:, 0]).at[:, gr:2 * gr, GLA_K:].set(gla_w_gate[:, 1])
    bgate = gla_b_gate.reshape(depth, 1, 2 * GLA_K)
    return wa, wq2, wk, wvt, wgate.astype(BF16), bgate


def _bucket_tables(counts, n_sorted_tiles):
    padded = ((counts + TM - 1) // TM) * TM
    ends = jnp.cumsum(padded)
    offsets = ends - padded
    n_used = (ends[-1] // TM).astype(jnp.int32)
    tile_start = jnp.arange(n_sorted_tiles, dtype=jnp.int32) * TM
    tile_bucket = jnp.sum((tile_start[:, None] >= ends[None, :]).astype(jnp.int32), axis=1)
    last_bucket = jnp.sum((jnp.maximum(ends[-1] - TM, 0) >= ends).astype(jnp.int32))
    tile_bucket = jnp.where(tile_start < ends[-1], tile_bucket, last_bucket)
    pair_lo = np.array([a for a in range(PER_GROUP) for b in range(a + 1, PER_GROUP)], np.int32)
    pair_hi = np.array([b for a in range(PER_GROUP) for b in range(a + 1, PER_GROUP)], np.int32)
    group = tile_bucket // N_PAIRS
    pair = tile_bucket % N_PAIRS
    e1 = group * PER_GROUP + jnp.asarray(pair_lo)[pair]
    e2 = group * PER_GROUP + jnp.asarray(pair_hi)[pair]
    return offsets.astype(jnp.int32), e1.astype(jnp.int32), e2.astype(jnp.int32), n_used.reshape(1)


def kernel(x, c, ctx, c_ctx, w_mod, b_mod, w_in, mla_q_norm, mla_wq_b, mla_kv_norm, mla_wkv_b, mla_w_o, conv_w,
           conv_b, conv_w_o, gla_w_gate, gla_b_gate, gla_norm, gla_w_o, w_out, ln1_g, ln1_b, router_w, router_b,
           exp_wg, exp_wu, exp_wd, ln2_g, ln2_b):
    bn, seq_len, d = x.shape
    ctx_len = ctx.shape[1]
    depth = w_in.shape[0]
    assert ctx_len == TM and seq_len % TM == 0 and d == D_MODEL and bn < 8
    alpha = (2 * depth) ** 0.25
    t = ctx_len + seq_len
    n_tok = bn * t
    n_sorted_tiles = n_tok // TM + N_BUCKETS

    cond = jnp.concatenate([c, c_ctx[None, :], jnp.zeros((7 - bn, d), F32)], axis=0)
    mods = _modulation(cond, w_mod, b_mod).reshape(depth, 8, 6, d)
    rope_tab = _rope_tables(ctx_len, seq_len)
    wa, wq2, wk, wvt, wgate, bgate = _prep_layer_weights(w_in, mla_wq_b, mla_wkv_b, gla_w_gate, gla_b_gate)
    row = lambda a: a.reshape(depth, 1, a.shape[-1])
    gq, gkv, cbias, gnorm = row(mla_q_norm), row(mla_kv_norm), row(conv_b), row(gla_norm)
    lng1, lnb1, lng2, lnb2 = row(ln1_g), row(ln1_b), row(ln2_g), row(ln2_b)
    wmla, wconv, wgla, wout = (a.astype(BF16) for a in (mla_w_o, conv_w_o, gla_w_o, w_out))
    rw = jnp.concatenate([router_w, jnp.zeros((d, LANES - N_EXPERTS), F32)], axis=1)
    rw_hi = rw.astype(BF16)
    rw_lo = (rw - rw_hi.astype(F32)).astype(BF16)
    rb = router_b.reshape(N_EXPERTS, 1)
    de = exp_wg.shape[-1]
    wg = exp_wg.astype(BF16).reshape(depth * N_EXPERTS, d, de)
    wu = exp_wu.astype(BF16).reshape(depth * N_EXPERTS, d, de)
    wd = exp_wd.astype(BF16).reshape(depth * N_EXPERTS, de, d)

    xt = jnp.concatenate([ctx, x], axis=1)
    for l in range(depth):
        q, k, vt, y, cb, gl, dec, sr, sg = _in_proj(xt, mods, l, rope_tab, wa[l], gq[l], wq2[l], gkv[l], wk[l],
                                                    wvt[l], wgate[l], bgate[l])
        att = _attention(q, k, vt)
        o_f, o_b = _gla(gl, dec)
        x1, u_tok, br, counts = _finish(xt, mods, l, y, cb, sr, sg, att, o_f, o_b, wmla[l], conv_w[l], cbias[l],
                                        wconv[l], gnorm[l], wgla[l], wout[l], lng1[l], lnb1[l], rw_hi, rw_lo, rb, alpha)
        counts = counts[:N_BUCKETS, 0]
        offsets, tile_e1, tile_e2, n_used = _bucket_tables(counts, n_sorted_tiles)
        br = br.reshape(bn * (t // TM), 8, TM)
        pos = _place(offsets, br[:, 0, :], br[:, 1, :])
        src = _invert(pos, offsets, counts, n_used, n_sorted_tiles)
        ys = _experts(tile_e1, tile_e2, n_used, src, u_tok, wg, wu, wd, l)
        xt = _ffn_out(pos, x1, mods, l, ys, lng2[l], lnb2[l], alpha, first_tile=int(l == depth - 1))
    return xt
```

```python
import functools

import numpy as np
import jax
import jax.numpy as jnp
from jax import lax
from jax.experimental import pallas as pl
from jax.experimental.pallas import tpu as pltpu

D_MODEL = 1024
GRID_W = 64
MLA_HEADS = 8
MLA_NOPE = 64
MLA_ROPE = 32
MLA_V = 64
MLA_Q_RANK = 256
MLA_KV_RANK = 128
ROPE_BASE = 10000.0
CONV_CH = 256
GLA_HEADS = 4
GLA_DK = 32
GLA_DV = 64
GLA_GATE_RANK = 16
GLA_TAU = 16.0
GLA_CHUNK = 64
N_EXPERTS = 16
N_GROUPS = 4
PER_GROUP = N_EXPERTS // N_GROUPS
D_EXPERT = 512
NORM_EPS = 1e-6
F32 = jnp.float32
BF16 = jnp.bfloat16

LANES = 128
TM = 256
HEAD_PAD = 128
QK_DIM = MLA_NOPE + MLA_ROPE
GLA_K = GLA_HEADS * GLA_DK
GLA_V = GLA_HEADS * GLA_DV
N_PAIRS = PER_GROUP * (PER_GROUP - 1) // 2
N_BUCKETS = N_GROUPS * N_PAIRS
BUCKET_ROWS = 32
TOKEN_ROWS = D_MODEL // LANES
XS_ROWS = 2 * TOKEN_ROWS
XS_USED = TOKEN_ROWS + 1
ISSUE_UNROLL = 8
ATTN_KEY_CHUNK = 256
VMEM_LIMIT = 56 * 1024 * 1024

C_CQ = 0
C_CKV = C_CQ + MLA_Q_RANK
C_KPE = C_CKV + MLA_KV_RANK
C_CONV = C_KPE + 2 * LANES
C_GLA = C_CONV + 3 * CONV_CH
C_GATES = C_GLA + 2 * GLA_K + 2 * GLA_V + LANES
W_A_COLS = C_GATES + 3 * D_MODEL


def _params(*sem):
    return pltpu.CompilerParams(dimension_semantics=sem, vmem_limit_bytes=VMEM_LIMIT)


def _const_spec(shape):
    n = len(shape)
    return pl.BlockSpec(shape, lambda *_: (0,) * n, pipeline_mode=pl.Buffered(1))


def _dot(a, b):
    return jnp.dot(a, b, preferred_element_type=F32)


def _dot_nt(a, b):
    return lax.dot_general(a, b, (((1,), (1,)), ((), ())), preferred_element_type=F32)


def _dot_tn(a, b):
    return lax.dot_general(a, b, (((0,), (0,)), ((), ())), preferred_element_type=F32)


def _split_hi_lo(x):
    hi = x.astype(BF16)
    lo = (x - hi.astype(F32)).astype(BF16)
    return hi, lo


def _sigmoid(x):
    return 1.0 / (1.0 + jnp.exp(-x))


def _silu(x):
    return x * _sigmoid(x)


def _layer_norm(v, g, b):
    mu = jnp.mean(v, axis=-1, keepdims=True)
    d = v - mu
    var = jnp.mean(d * d, axis=-1, keepdims=True)
    return d * lax.rsqrt(var + NORM_EPS) * g + b


def _rms(v, g):
    return v * lax.rsqrt(jnp.mean(v * v, axis=-1, keepdims=True) + NORM_EPS) * g


def _mod_kernel(c_ref, w_ref, b_ref, o_ref):
    o_ref[...] = _dot(_silu(c_ref[...]).astype(BF16), w_ref[...].astype(BF16)) + b_ref[...]


def _modulation(cond, w_mod, b_mod):
    depth, d, n = w_mod.shape
    tn = n // 4
    return pl.pallas_call(
        _mod_kernel,
        grid=(depth, n // tn),
        in_specs=[pl.BlockSpec((8, d), lambda l, j: (0, 0)),
                  pl.BlockSpec((None, d, tn), lambda l, j: (l, 0, j)),
                  pl.BlockSpec((None, 1, tn), lambda l, j: (l, 0, j))],
        out_specs=pl.BlockSpec((None, 8, tn), lambda l, j: (l, 0, j)),
        out_shape=jax.ShapeDtypeStruct((depth, 8, n), F32),
        compiler_params=_params("parallel", "parallel"),
        name="modulation",
    )(cond, w_mod, b_mod.reshape(depth, 1, n))


def _in_proj_kernel(x_ref, mod_ref, rope_ref, wa_ref, gq_ref, wq_ref, gkv_ref, wk_ref, wvt_ref, wgate_ref, bgate_ref,
                    q_ref, k_ref, vt_ref, y_ref, cb_ref, gl_ref, dec_ref, sr_ref, sg_ref):
    u = (x_ref[...] * (1.0 + mod_ref[1:2, :]) + mod_ref[0:1, :]).astype(BF16)
    cq_tab, sq_tab = rope_ref[:, 0:LANES], rope_ref[:, LANES:2 * LANES]
    ck_tab, sk_tab = rope_ref[:, 2 * LANES:3 * LANES], rope_ref[:, 3 * LANES:4 * LANES]

    cq = _dot(u, wa_ref[:, C_CQ:C_CKV])
    q2 = _dot(_rms(cq, gq_ref[...]).astype(BF16), wq_ref[...])
    nq = MLA_HEADS * HEAD_PAD
    for h in range(MLA_HEADS):
        a, b = h * HEAD_PAD, (h + 1) * HEAD_PAD
        q_ref[:, a:b] = (q2[:, a:b] * cq_tab + q2[:, nq + a:nq + b] * sq_tab).astype(BF16)

    ckv = _dot(u, wa_ref[:, C_CKV:C_KPE])
    ckvn = _rms(ckv, gkv_ref[...]).astype(BF16)
    kn = _dot(ckvn, wk_ref[...])
    kpe2 = _dot(u, wa_ref[:, C_KPE:C_CONV])
    kpe = kpe2[:, 0:LANES] * ck_tab + kpe2[:, LANES:2 * LANES] * sk_tab
    for h in range(MLA_HEADS):
        a, b = h * HEAD_PAD, (h + 1) * HEAD_PAD
        k_ref[:, a:b] = (kn[:, a:b] + kpe).astype(BF16)
    vt = _dot_nt(wvt_ref[...], ckvn)
    vrow = lax.broadcasted_iota(jnp.int32, vt.shape, 0) % HEAD_PAD
    vt_ref[...] = jnp.where(vrow == MLA_V, 1.0, vt).astype(BF16)

    cv = _dot(u, wa_ref[:, C_CONV:C_GLA])
    cb_ref[...] = cv[:, 0:CONV_CH]
    y_ref[...] = cv[:, CONV_CH:2 * CONV_CH] * cv[:, 2 * CONV_CH:3 * CONV_CH]

    g = _dot(u, wa_ref[:, C_GLA:C_GATES])
    gq = g[:, 0:GLA_K] * (GLA_DK ** -0.5)
    gk = g[:, GLA_K:2 * GLA_K]
    gv = g[:, 2 * GLA_K:2 * GLA_K + GLA_V]
    gr = g[:, 2 * GLA_K + GLA_V:2 * GLA_K + 2 * GLA_V]
    low = g[:, 2 * GLA_K + 2 * GLA_V:]
    pre = _dot(low.astype(BF16), wgate_ref[...]) + bgate_ref[...]
    logg = (jnp.minimum(pre, 0.0) - jnp.log(1.0 + jnp.exp(-jnp.abs(pre)))) * (1.0 / GLA_TAU)
    lf, lb = logg[:, 0:GLA_K], logg[:, GLA_K:2 * GLA_K]
    tm = lf.shape[0]
    ri = lax.broadcasted_iota(jnp.int32, (tm, tm), 0)
    ci = lax.broadcasted_iota(jnp.int32, (tm, tm), 1)
    same = (ri // GLA_CHUNK) == (ci // GLA_CHUNK)
    m_low = jnp.where(same & (ci <= ri), 1.0, 0.0).astype(BF16)
    m_up = jnp.where(same & (ci >= ri), 1.0, 0.0).astype(BF16)
    pieces = jnp.concatenate(_split_hi_lo(lf) + _split_hi_lo(lb), axis=1)
    pm = _dot(m_low, pieces)
    pu = _dot(m_up, pieces)
    pre_f = pm[:, 0:GLA_K] + pm[:, GLA_K:2 * GLA_K]
    pre_b = pm[:, 2 * GLA_K:3 * GLA_K] + pm[:, 3 * GLA_K:4 * GLA_K]
    suf_f = pu[:, 0:GLA_K] + pu[:, GLA_K:2 * GLA_K]
    suf_b = pu[:, 2 * GLA_K:3 * GLA_K] + pu[:, 3 * GLA_K:4 * GLA_K]
    gl_ref[:, 0:GLA_K] = (gq * jnp.exp(pre_f)).astype(BF16)
    gl_ref[:, GLA_K:2 * GLA_K] = (gk * jnp.exp(-pre_f)).astype(BF16)
    gl_ref[:, 2 * GLA_K:3 * GLA_K] = (gk * jnp.exp(suf_f - lf)).astype(BF16)
    gl_ref[:, 3 * GLA_K:4 * GLA_K] = (gq * jnp.exp(suf_b)).astype(BF16)
    gl_ref[:, 4 * GLA_K:5 * GLA_K] = (gk * jnp.exp(-suf_b)).astype(BF16)
    gl_ref[:, 5 * GLA_K:6 * GLA_K] = (gk * jnp.exp(pre_b - lb)).astype(BF16)
    gl_ref[:, 6 * GLA_K:6 * GLA_K + GLA_V] = gv.astype(BF16)
    dec_ref[:, 0:GLA_K] = jnp.exp(pre_f + suf_f - lf)
    dec_ref[:, GLA_K:2 * GLA_K] = jnp.exp(pre_b + suf_b - lb)
    sr_ref[...] = _silu(gr)

    for j in range(3):
        a, b = j * D_MODEL, (j + 1) * D_MODEL
        sg_ref[:, a:b] = _sigmoid(_dot(u, wa_ref[:, C_GATES + a:C_GATES + b])).astype(BF16)


def _in_proj(x, mods, l, rope_tab, wa, gq, wq, gkv, wk, wvt, wgate, bgate):
    bn, t, d = x.shape
    nt = t // TM
    tok = lambda w: pl.BlockSpec((None, TM, w), lambda b, i: (b, i, 0))
    outs = [(MLA_HEADS * HEAD_PAD, BF16), (MLA_HEADS * HEAD_PAD, BF16), None,
            (CONV_CH, F32), (CONV_CH, F32), (6 * GLA_K + GLA_V, BF16), (2 * GLA_K, F32), (GLA_V, F32),
            (3 * D_MODEL, BF16)]
    vt_rows = MLA_HEADS * HEAD_PAD
    vt_spec = pl.BlockSpec((None, vt_rows, TM), lambda b, i: (b, 0, i))
    vt_shape = jax.ShapeDtypeStruct((bn, vt_rows, t), BF16)
    consts = [wa, gq, wq, gkv, wk, wvt, wgate, bgate]
    return pl.pallas_call(
        _in_proj_kernel,
        grid=(bn, nt),
        in_specs=[tok(d),
                  pl.BlockSpec((None, None, 6, d), lambda b, i: (l, jnp.where(i == 0, bn, b), 0, 0)),
                  pl.BlockSpec((TM, 4 * LANES), lambda b, i: (i, 0))]
                 + [_const_spec(a.shape) for a in consts],
        out_specs=[vt_spec if o is None else tok(o[0]) for o in outs],
        out_shape=[vt_shape if o is None else jax.ShapeDtypeStruct((bn, t, o[0]), o[1]) for o in outs],
        compiler_params=_params("parallel", "parallel"),
        name="in_proj",
    )(x, mods, rope_tab, *consts)


def _attn_kernel(q_ref, k_ref, vt_ref, o_ref, s0_ref, s1_ref, *, ctx_len):
    x_refs = (s0_ref, s1_ref)
    kc = ATTN_KEY_CHUNK

    def attend(nk):
        nc = nk // kc
        accs = []
        for j in range(2):
            q = q_ref[:, j * HEAD_PAD:(j + 1) * HEAD_PAD]
            m_run, m_at = None, []
            for c in range(nc):
                st = _dot_nt(k_ref[c * kc:(c + 1) * kc, j * HEAD_PAD:(j + 1) * HEAD_PAD], q)
                mc = jnp.max(st, axis=0, keepdims=True)
                m_run = mc if m_run is None else jnp.maximum(m_run, mc)
                x_refs[j][c * kc:(c + 1) * kc, :] = (st - m_run).astype(BF16)
                m_at.append(m_run)
            acc = None
            for c in range(nc):
                p = jnp.exp2(x_refs[j][c * kc:(c + 1) * kc, :])
                part = _dot(vt_ref[j * HEAD_PAD:(j + 1) * HEAD_PAD, c * kc:(c + 1) * kc], p)
                if c + 1 < nc:
                    part = part * jnp.exp2(m_at[c] - m_run)
                acc = part if acc is None else acc + part
            accs.append(acc)
        outs = [a[0:MLA_V, :] / a[MLA_V:MLA_V + 1, :] for a in accs]
        o_ref[...] = jnp.concatenate(outs, axis=0).T.astype(o_ref.dtype)

    is_ctx = pl.program_id(2) == 0
    pl.when(is_ctx)(lambda: attend(ctx_len))
    pl.when(jnp.logical_not(is_ctx))(lambda: attend(k_ref.shape[0]))


def _attention(q, k, vt):
    bn, t, _ = q.shape
    hp = MLA_HEADS // 2
    return pl.pallas_call(
        functools.partial(_attn_kernel, ctx_len=TM),
        grid=(bn, hp, t // TM),
        in_specs=[pl.BlockSpec((None, TM, 2 * HEAD_PAD), lambda b, h, i: (b, i, h)),
                  pl.BlockSpec((None, t, 2 * HEAD_PAD), lambda b, h, i: (b, 0, h)),
                  pl.BlockSpec((None, 2 * HEAD_PAD, t), lambda b, h, i: (b, h, 0))],
        out_specs=pl.BlockSpec((None, TM, 2 * MLA_V), lambda b, h, i: (b, i, h)),
        out_shape=jax.ShapeDtypeStruct((bn, t, MLA_HEADS * MLA_V), BF16),
        scratch_shapes=[pltpu.VMEM((t, TM), BF16)] * 2,
        compiler_params=_params("parallel", "parallel", "arbitrary"),
        name="attention",
    )(q, k, vt)


def _gla_kernel(qf_ref, vf_ref, df_ref, qb_ref, vb_ref, db_ref, of_ref, ob_ref, sf_ref, sb_ref):
    @pl.when(pl.program_id(1) == 0)
    def _():
        sf_ref[...] = jnp.zeros_like(sf_ref)
        sb_ref[...] = jnp.zeros_like(sb_ref)

    c = GLA_CHUNK
    lane_k = lax.broadcasted_iota(jnp.int32, (c, GLA_K), 1) // GLA_DK
    lane_v = lax.broadcasted_iota(jnp.int32, (c, GLA_V), 1) // GLA_DV
    ai = lax.broadcasted_iota(jnp.int32, (c, GLA_HEADS * c), 0)
    aj = lax.broadcasted_iota(jnp.int32, (c, GLA_HEADS * c), 1) % c
    state_mask = (lax.broadcasted_iota(jnp.int32, (GLA_V, GLA_K), 0) // GLA_DV ==
                  lax.broadcasted_iota(jnp.int32, (GLA_V, GLA_K), 1) // GLA_DK)
    zero = jnp.zeros((), BF16)

    def chunk(g_ref, v_ref, d_ref, o_ref, s_ref, b, r, causal):
        rows = slice(r * c, (r + 1) * c)
        qd = g_ref[b, rows, 0:GLA_K]
        kd = g_ref[b, rows, GLA_K:2 * GLA_K]
        kend = g_ref[b, rows, 2 * GLA_K:3 * GLA_K]
        v = v_ref[b, rows, :]
        dec = d_ref[b, r * c:r * c + 1, :]
        kd_heads = jnp.concatenate([jnp.where(lane_k == h, kd, zero) for h in range(GLA_HEADS)], axis=0)
        v_heads = jnp.concatenate([jnp.where(lane_v == h, v, zero) for h in range(GLA_HEADS)], axis=0)
        att = jnp.where(causal, _dot_nt(qd, kd_heads), 0.0)
        st = s_ref[b]
        o_ref[b, rows, :] = _dot(att.astype(BF16), v_heads) + _dot_nt(qd, st.astype(BF16))
        s_ref[b] = st * dec + jnp.where(state_mask, _dot_tn(v, kend), 0.0)

    n_chunks = qf_ref.shape[1] // c
    for r in range(n_chunks):
        for b in range(qf_ref.shape[0]):
            chunk(qf_ref, vf_ref, df_ref, of_ref, sf_ref, b, r, aj <= ai)
            chunk(qb_ref, vb_ref, db_ref, ob_ref, sb_ref, b, n_chunks - 1 - r, aj >= ai)


def _gla(gl, dec):
    bn, t, _ = gl.shape
    nt = t // TM
    gb = 2 if bn % 2 == 0 else 1
    fwd = lambda i: i
    bwd = lambda i: jnp.where(i == 0, 0, nt - i)
    spec = lambda w, order, col: pl.BlockSpec((gb, TM, w), lambda b, i: (b, order(i), col))
    return pl.pallas_call(
        _gla_kernel,
        grid=(bn // gb, nt),
        in_specs=[spec(3 * GLA_K, fwd, 0), spec(GLA_V, fwd, 3), spec(GLA_K, fwd, 0),
                  spec(3 * GLA_K, bwd, 1), spec(GLA_V, bwd, 3), spec(GLA_K, bwd, 1)],
        out_specs=[spec(GLA_V, fwd, 0), spec(GLA_V, bwd, 0)],
        out_shape=[jax.ShapeDtypeStruct((bn, t, GLA_V), F32)] * 2,
        scratch_shapes=[pltpu.VMEM((gb, GLA_V, GLA_K), F32)] * 2,
        compiler_params=_params("parallel", "arbitrary"),
        name="gla",
    )(gl, gl, dec, gl, gl, dec)


def _finish_kernel(x_ref, mod_ref, y_ref, yp_ref, yn_ref, cb_ref, sr_ref, sg_ref, att_ref, of_ref, ob_ref,
                   wmla_ref, cw_ref, cbias_ref, wconv_ref, gnorm_ref, wgla_ref, wout_ref, lng_ref, lnb_ref,
                   rwh_ref, rwl_ref, rb_ref, o_ref, u_ref, br_ref, cnt_ref, base_ref, *, alpha):
    i = pl.program_id(1)
    nt = pl.num_programs(1)
    tm = y_ref.shape[0]
    y = y_ref[...]
    has_prev = jnp.logical_and(i != 0, i != 1)
    has_next = jnp.logical_and(i != 0, i != nt - 1)
    prev_row = jnp.where(has_prev, yp_ref[7:8, :], 0.0)
    next_row = jnp.where(has_next, yn_ref[0:1, :], 0.0)
    row = lax.broadcasted_iota(jnp.int32, y.shape, 0)
    y_m1 = jnp.where(row == 0, prev_row, pltpu.roll(y, 1, axis=0))
    y_p1 = jnp.where(row == tm - 1, next_row, pltpu.roll(y, tm - 1, axis=0))
    conv = cbias_ref[...] + y_m1 * cw_ref[0:1, :] + y * cw_ref[1:2, :] + y_p1 * cw_ref[2:3, :]
    conv = cb_ref[...] * conv

    o = of_ref[...] + ob_ref[...]
    gi = lax.broadcasted_iota(jnp.int32, (GLA_V, GLA_V), 0) // GLA_DV
    gj = lax.broadcasted_iota(jnp.int32, (GLA_V, GLA_V), 1) // GLA_DV
    grp = jnp.where(gi == gj, 1.0 / GLA_DV, 0.0).astype(BF16)
    hi, lo = _split_hi_lo(o * o)
    ms = _dot(hi, grp) + _dot(lo, grp)
    gla = o * lax.rsqrt(ms + NORM_EPS) * gnorm_ref[...] * sr_ref[...]

    d = x_ref.shape[1]
    merged = (sg_ref[:, 0:d].astype(F32) * _dot(att_ref[...], wmla_ref[...])
              + sg_ref[:, d:2 * d].astype(F32) * _dot(conv.astype(BF16), wconv_ref[...])
              + sg_ref[:, 2 * d:3 * d].astype(F32) * _dot(gla.astype(BF16), wgla_ref[...]))
    m = _dot(merged.astype(BF16), wout_ref[...])
    x1 = _layer_norm(alpha * x_ref[...] + mod_ref[2:3, :] * m, lng_ref[...], lnb_ref[...])
    o_ref[...] = x1
    _route_tile(x1, mod_ref, rwh_ref, rwl_ref, rb_ref, u_ref, br_ref, cnt_ref, base_ref)


def _finish(x, mods, l, y, cb, sr, sg, att, o_f, o_b, wmla, cw, cbias, wconv, gnorm, wgla, wout, lng, lnb, rw_hi,
            rw_lo, rb, alpha):
    bn, t, d = x.shape
    nt = t // TM
    tok = lambda w: pl.BlockSpec((None, TM, w), lambda b, i: (b, i, 0))
    rows8 = TM // 8
    consts = [wmla, cw, cbias, wconv, gnorm, wgla, wout, lng, lnb, rw_hi, rw_lo, rb]
    return pl.pallas_call(
        functools.partial(_finish_kernel, alpha=alpha),
        grid=(bn, nt),
        in_specs=[tok(d),
                  pl.BlockSpec((None, None, 6, d), lambda b, i: (l, jnp.where(i == 0, bn, b), 0, 0)),
                  tok(CONV_CH),
                  pl.BlockSpec((None, 8, CONV_CH), lambda b, i: (b, jnp.maximum(i * rows8 - 1, 0), 0)),
                  pl.BlockSpec((None, 8, CONV_CH), lambda b, i: (b, jnp.minimum((i + 1) * rows8, nt * rows8 - 1), 0)),
                  tok(CONV_CH), tok(GLA_V), tok(3 * d), tok(MLA_HEADS * MLA_V), tok(GLA_V), tok(GLA_V)]
                 + [_const_spec(a.shape) for a in consts],
        out_specs=[tok(d),
                   pl.BlockSpec((TM * XS_ROWS, LANES), lambda b, i: (b * nt + i, 0)),
                   pl.BlockSpec((None, None, 8, TM), lambda b, i: (b, i, 0, 0)),
                   pl.BlockSpec((BUCKET_ROWS, LANES), lambda b, i: (0, 0))],
        out_shape=[jax.ShapeDtypeStruct((bn, t, d), F32),
                   jax.ShapeDtypeStruct((bn * t * XS_ROWS, LANES), F32),
                   jax.ShapeDtypeStruct((bn, nt, 8, TM), jnp.int32),
                   jax.ShapeDtypeStruct((BUCKET_ROWS, LANES), jnp.int32)],
        scratch_shapes=[pltpu.VMEM((BUCKET_ROWS, LANES), F32)],
        compiler_params=_params("arbitrary", "arbitrary"),
        name="finish",
    )(x, mods, y, y, y, cb, sr, sg, att, o_f, o_b, *consts)


def _route_tile(x1, mod_ref, rwh_ref, rwl_ref, rb_ref, u_ref, br_ref, cnt_ref, base_ref):
    first = jnp.logical_and(pl.program_id(0) == 0, pl.program_id(1) == 0)

    @pl.when(first)
    def _():
        base_ref[...] = jnp.zeros_like(base_ref)

    tm = x1.shape[0]
    u = x1 * (1.0 + mod_ref[4:5, :]) + mod_ref[3:4, :]
    hi, lo = _split_hi_lo(u)
    for j in range(TOKEN_ROWS):
        u_ref[pl.ds(j, tm, stride=XS_ROWS), :] = u[:, j * LANES:(j + 1) * LANES]
    logits = _dot(hi, rwh_ref[...]) + _dot(lo, rwh_ref[...]) + _dot(hi, rwl_ref[...])
    s = _sigmoid(logits.T[0:N_EXPERTS, :])
    sel = s + rb_ref[...]

    sel_e = [sel[e:e + 1, :] for e in range(N_EXPERTS)]
    chosen, gscore = [], []
    for g in range(N_GROUPS):
        members = list(range(g * PER_GROUP, (g + 1) * PER_GROUP))
        picked = []
        for a in members:
            beaten = jnp.zeros((1, tm), F32)
            for b in members:
                if b != a:
                    wins = (sel_e[b] >= sel_e[a]) if b < a else (sel_e[b] > sel_e[a])
                    beaten = beaten + jnp.where(wins, 1.0, 0.0)
            picked.append(beaten < 2.0)
        chosen.append(picked)
        score = jnp.zeros((1, tm), F32)
        for a, p in zip(members, picked):
            score = score + jnp.where(p, sel_e[a], 0.0)
        gscore.append(score)
    best = []
    for g in range(N_GROUPS):
        ok = jnp.ones((1, tm), jnp.bool_)
        for h in range(N_GROUPS):
            if h < g:
                ok = jnp.logical_and(ok, gscore[g] > gscore[h])
            elif h > g:
                ok = jnp.logical_and(ok, gscore[g] >= gscore[h])
        best.append(ok)

    zero = jnp.zeros((1, tm), F32)
    w_lo, w_hi = zero, zero
    bucket_rows = []
    for g in range(N_GROUPS):
        taken = [jnp.logical_and(best[g], p) for p in chosen[g]]
        seen = jnp.zeros((1, tm), jnp.bool_)
        for a in range(PER_GROUP):
            sa = s[g * PER_GROUP + a:g * PER_GROUP + a + 1, :]
            w_lo = w_lo + jnp.where(jnp.logical_and(taken[a], jnp.logical_not(seen)), sa, 0.0)
            w_hi = w_hi + jnp.where(jnp.logical_and(taken[a], seen), sa, 0.0)
            seen = jnp.logical_or(seen, taken[a])
        for a in range(PER_GROUP):
            for b in range(a + 1, PER_GROUP):
                bucket_rows.append(jnp.where(jnp.logical_and(taken[a], taken[b]), 1.0, 0.0))
    total = w_lo + w_hi
    w_lo, w_hi = w_lo / total, w_hi / total
    wts = jnp.concatenate([w_lo, w_hi, jnp.zeros((LANES - 2, tm), F32)], axis=0)
    u_ref[pl.ds(TOKEN_ROWS, tm, stride=XS_ROWS), :] = wts.T
    for j in range(TOKEN_ROWS + 1, XS_ROWS):
        u_ref[pl.ds(j, tm, stride=XS_ROWS), :] = jnp.zeros((tm, LANES), F32)

    onehot = jnp.concatenate(bucket_rows + [jnp.zeros((BUCKET_ROWS - N_BUCKETS, tm), F32)], axis=0)
    ri = lax.broadcasted_iota(jnp.int32, (tm, tm), 0)
    ci = lax.broadcasted_iota(jnp.int32, (tm, tm), 1)
    before = _dot(onehot.astype(BF16), jnp.where(ri < ci, 1.0, 0.0).astype(BF16))
    base = base_ref[:, 0:1]
    rank = jnp.sum(onehot * (before + base), axis=0, keepdims=True)
    bidx = lax.broadcasted_iota(jnp.int32, onehot.shape, 0).astype(F32)
    bucket = jnp.sum(onehot * bidx, axis=0, keepdims=True)
    br_ref[...] = jnp.concatenate([bucket, rank, jnp.zeros((6, tm), F32)], axis=0).astype(jnp.int32)
    new_base = base + jnp.sum(onehot, axis=1, keepdims=True)
    base_ref[...] = jnp.broadcast_to(new_base, base_ref.shape)
    cnt_ref[...] = jnp.broadcast_to(new_base, cnt_ref.shape).astype(jnp.int32)


def _place_kernel(off_ref, bucket_ref, rank_ref, pos_ref):
    bucket = bucket_ref[...]
    pos = rank_ref[...]
    for b in range(N_BUCKETS):
        pos = pos + jnp.where(bucket == b, off_ref[b], 0)
    pos_ref[...] = pos


def _place(offsets, bucket, rank):
    return pl.pallas_call(
        _place_kernel,
        in_specs=[pl.BlockSpec(memory_space=pltpu.SMEM), pl.BlockSpec(memory_space=pltpu.VMEM),
                  pl.BlockSpec(memory_space=pltpu.VMEM)],
        out_specs=pl.BlockSpec(memory_space=pltpu.VMEM),
        out_shape=jax.ShapeDtypeStruct(bucket.shape, jnp.int32),
        name="place",
    )(offsets, bucket, rank)


def _slot_copy(src_ref, src_slot, dst_ref, dst_slot, sem, pitch, rows=None):
    rows = pitch if rows is None else rows
    return pltpu.make_async_copy(src_ref.at[pl.ds(pl.multiple_of(src_slot * pitch, pitch), rows)],
                                 dst_ref.at[pl.ds(pl.multiple_of(dst_slot * pitch, pitch), rows)], sem)


def _invert_kernel(pos_ref, off_ref, cnt_ref, used_ref, src_ref):
    def point_at_zero(p, carry):
        src_ref[p] = 0
        return carry

    for b in range(N_BUCKETS):
        start = off_ref[b] + cnt_ref[b]
        lax.fori_loop(start, (start + (TM - 1)) & (-TM), point_at_zero, 0)
    def unused_tile(i, carry):
        for r in range(TM):
            src_ref[i * TM + r] = 0
        return carry

    lax.fori_loop(used_ref[0], src_ref.shape[0] // TM, unused_tile, 0)

    def one(t, carry):
        src_ref[pos_ref[t]] = t
        return carry

    lax.fori_loop(0, pos_ref.shape[0], one, 0, unroll=ISSUE_UNROLL)


def _invert(pos, offsets, counts, n_used, n_sorted_tiles):
    smem = pl.BlockSpec(memory_space=pltpu.SMEM)
    return pl.pallas_call(
        _invert_kernel,
        in_specs=[smem, smem, smem, smem],
        out_specs=smem,
        out_shape=jax.ShapeDtypeStruct((n_sorted_tiles * TM,), jnp.int32),
        name="invert",
    )(pos.reshape(-1), offsets, counts, n_used).reshape(n_sorted_tiles, TM)


def _expert_kernel(e1_ref, e2_ref, used_ref, src_ref, nxt_ref, u_ref, wg1_ref, wu1_ref, wd1_ref, wg2_ref, wu2_ref,
                   wd2_ref, y_ref, buf0_ref, buf1_ref, sem):
    del e1_ref, e2_ref
    i = pl.program_id(0)
    n_used = used_ref[0]
    tm = buf0_ref.shape[0] // XS_ROWS
    bufs = (buf0_ref, buf1_ref)

    @pl.when(i == 0)
    def _():
        def issue(r, carry):
            _slot_copy(u_ref, src_ref[0, r], buf0_ref, r, sem.at[0], XS_ROWS, XS_USED).start()
            return carry
        lax.fori_loop(0, tm, issue, 0, unroll=ISSUE_UNROLL)

    def run(s):
        cur, nxt = bufs[s], bufs[1 - s]

        @pl.when(i < n_used)
        def _():
            for r in range(tm):
                _slot_copy(u_ref, nxt_ref[0, r], nxt, r, sem.at[1 - s], XS_ROWS, XS_USED).start()

        pltpu.make_async_copy(u_ref.at[pl.ds(0, tm * XS_USED)], cur.at[pl.ds(0, tm * XS_USED)], sem.at[s]).wait()

        @pl.when(i < n_used)
        def _():
            x = jnp.concatenate([cur[pl.ds(j, tm, stride=XS_ROWS), :].astype(BF16) for j in range(TOKEN_ROWS)], axis=1)
            wts = cur[pl.ds(TOKEN_ROWS, tm, stride=XS_ROWS), :]
            h1 = _silu(_dot(x, wg1_ref[...])) * _dot(x, wu1_ref[...])
            h2 = _silu(_dot(x, wg2_ref[...])) * _dot(x, wu2_ref[...])
            y = (wts[:, 0:1] * _dot(h1.astype(BF16), wd1_ref[...])
                 + wts[:, 1:2] * _dot(h2.astype(BF16), wd2_ref[...]))
            for j in range(TOKEN_ROWS):
                y_ref[pl.ds(j, tm, stride=TOKEN_ROWS), :] = y[:, j * LANES:(j + 1) * LANES]

    requested = i <= n_used
    pl.when(jnp.logical_and(requested, i % 2 == 0))(lambda: run(0))
    pl.when(jnp.logical_and(requested, i % 2 == 1))(lambda: run(1))

    @pl.when(i >= n_used)
    def _():
        y_ref[...] = jnp.zeros_like(y_ref)


def _experts(tile_e1, tile_e2, n_used, src, u_tok, wg, wu, wd, l):
    n_tiles = src.shape[0]
    d, de = wg.shape[1], wg.shape[2]
    up = lambda tbl: pl.BlockSpec((None, d, de), lambda i, e1, e2, n: (l * N_EXPERTS + tbl(e1, e2)[i], 0, 0))
    down = lambda tbl: pl.BlockSpec((None, de, d), lambda i, e1, e2, n: (l * N_EXPERTS + tbl(e1, e2)[i], 0, 0))
    first = lambda e1, e2: e1
    second = lambda e1, e2: e2
    src3 = src.reshape(n_tiles, 1, TM)
    return pl.pallas_call(
        _expert_kernel,
        grid_spec=pltpu.PrefetchScalarGridSpec(
            num_scalar_prefetch=3,
            grid=(n_tiles,),
            in_specs=[pl.BlockSpec((None, 1, TM), lambda i, e1, e2, n: (i, 0, 0), memory_space=pltpu.SMEM),
                      pl.BlockSpec((None, 1, TM), lambda i, e1, e2, n: (jnp.minimum(i + 1, n_tiles - 1), 0, 0),
                                   memory_space=pltpu.SMEM),
                      pl.BlockSpec(memory_space=pl.ANY),
                      up(first), up(first), down(first), up(second), up(second), down(second)],
            out_specs=pl.BlockSpec((TM * TOKEN_ROWS, LANES), lambda i, e1, e2, n: (i, 0)),
            scratch_shapes=[pltpu.VMEM((TM * XS_ROWS, LANES), F32), pltpu.VMEM((TM * XS_ROWS, LANES), F32),
                            pltpu.SemaphoreType.DMA((2,))],
        ),
        out_shape=jax.ShapeDtypeStruct((n_tiles * TM * TOKEN_ROWS, LANES), F32),
        compiler_params=_params("arbitrary"),
        name="experts",
    )(tile_e1, tile_e2, n_used, src3, src3, u_tok, wg, wu, wd, wg, wu, wd)


def _ffn_out_kernel(pos_ref, nxt_ref, x_ref, mod_ref, ys_ref, lng_ref, lnb_ref, o_ref, buf_ref, sem, *, alpha):
    tm = x_ref.shape[0]
    step = pl.program_id(0) * pl.num_programs(1) + pl.program_id(1)
    n_steps = pl.num_programs(0) * pl.num_programs(1)
    slot = step % 2

    @pl.when(step == 0)
    def _():
        def issue(r, carry):
            _slot_copy(ys_ref, pos_ref[0, r], buf_ref.at[0], r, sem.at[0], TOKEN_ROWS).start()
            return carry
        lax.fori_loop(0, tm, issue, 0, unroll=ISSUE_UNROLL)

    @pl.when(step + 1 < n_steps)
    def _():
        for r in range(tm):
            _slot_copy(ys_ref, nxt_ref[0, r], buf_ref.at[1 - slot], r, sem.at[1 - slot], TOKEN_ROWS).start()
    cur = buf_ref.at[slot]
    pltpu.make_async_copy(ys_ref.at[pl.ds(0, tm * TOKEN_ROWS)], cur, sem.at[slot]).wait()
    f = jnp.concatenate([cur[pl.ds(j, tm, stride=TOKEN_ROWS), :] for j in range(TOKEN_ROWS)], axis=1)
    o_ref[...] = _layer_norm(alpha * x_ref[...] + mod_ref[5:6, :] * f, lng_ref[...], lnb_ref[...])


def _ffn_out(pos, x1, mods, l, ys, lng, lnb, alpha, first_tile):
    bn, t, d = x1.shape
    nt = t // TM
    ng = nt - first_tile
    pos3 = pos.reshape(bn * nt, 1, TM)

    def next_tile(b, i):
        s = jnp.minimum(b * ng + i + 1, bn * ng - 1)
        return ((s // ng) * nt + s % ng + first_tile, 0, 0)

    return pl.pallas_call(
        functools.partial(_ffn_out_kernel, alpha=alpha),
        grid=(bn, ng),
        in_specs=[pl.BlockSpec((None, 1, TM), lambda b, i: (b * nt + i + first_tile, 0, 0), memory_space=pltpu.SMEM),
                  pl.BlockSpec((None, 1, TM), next_tile, memory_space=pltpu.SMEM),
                  pl.BlockSpec((None, TM, d), lambda b, i: (b, i + first_tile, 0)),
                  pl.BlockSpec((None, None, 6, d), lambda b, i: (l, jnp.where(i + first_tile == 0, bn, b), 0, 0)),
                  pl.BlockSpec(memory_space=pl.ANY),
                  _const_spec(lng.shape), _const_spec(lnb.shape)],
        out_specs=pl.BlockSpec((None, TM, d), lambda b, i: (b, i, 0)),
        out_shape=jax.ShapeDtypeStruct((bn, ng * TM, d), F32),
        scratch_shapes=[pltpu.VMEM((2, TM * TOKEN_ROWS, LANES), F32), pltpu.SemaphoreType.DMA((2,))],
        compiler_params=_params("arbitrary", "arbitrary"),
        name="ffn_out",
    )(pos3, pos3, x1, mods, ys, lng, lnb)


def _rope_tables(ctx_len, seq_len):
    t = np.arange(seq_len)
    half = MLA_ROPE // 2
    inv_freq = 1.0 / (ROPE_BASE ** (np.arange(0, half, 2, dtype=np.float32) / half))
    ang_r = (t // GRID_W).astype(np.float32)[:, None] * inv_freq[None, :]
    ang_c = (t % GRID_W).astype(np.float32)[:, None] * inv_freq[None, :]
    ang = np.concatenate([ang_r, ang_r, ang_c, ang_c], axis=-1).astype(np.float32)
    cos = np.concatenate([np.ones((ctx_len, MLA_ROPE), np.float32), np.cos(ang)], axis=0)
    sin = np.concatenate([np.zeros((ctx_len, MLA_ROPE), np.float32), np.sin(ang)], axis=0)
    n = ctx_len + seq_len
    c_tab = np.zeros((n, LANES), np.float32)
    s_tab = np.zeros((n, LANES), np.float32)
    c_tab[:, :MLA_NOPE] = 1.0
    c_tab[:, MLA_NOPE:QK_DIM] = cos
    s_tab[:, MLA_NOPE:QK_DIM] = sin
    scale = np.float32(QK_DIM ** -0.5 * np.log2(np.e))
    return jnp.asarray(np.concatenate([c_tab * scale, s_tab * scale, c_tab, s_tab], axis=1))


def _rotate_cols(w):
    q = MLA_ROPE // 4
    w1, w2, w3, w4 = (w[..., i * q:(i + 1) * q] for i in range(4))
    return jnp.concatenate([-w2, w1, -w4, w3], axis=-1)


def _prep_layer_weights(w_in, mla_wq_b, mla_wkv_b, gla_w_gate, gla_b_gate):
    depth, d, _ = w_in.shape
    splits = np.cumsum([0, MLA_Q_RANK, MLA_KV_RANK, MLA_ROPE, CONV_CH, CONV_CH, CONV_CH, GLA_K, GLA_K, GLA_V,
                        GLA_V, GLA_GATE_RANK, GLA_GATE_RANK, 3 * D_MODEL])
    seg = [w_in[:, :, splits[i]:splits[i + 1]] for i in range(13)]
    zeros = lambda *s: jnp.zeros(s, F32)
    pad_rope = lambda w: jnp.concatenate([zeros(depth, d, MLA_NOPE), w, zeros(depth, d, LANES - QK_DIM)], axis=-1)
    wa = jnp.concatenate(
        [seg[0], seg[1], pad_rope(seg[2]), pad_rope(_rotate_cols(seg[2])), seg[3], seg[4], seg[5], seg[6], seg[7],
         seg[8], seg[9], seg[10], seg[11], zeros(depth, d, LANES - 2 * GLA_GATE_RANK), seg[12]], axis=-1).astype(BF16)

    r = mla_wq_b.shape[1]
    wq = mla_wq_b.reshape(depth, r, MLA_HEADS, QK_DIM)
    q_main = jnp.concatenate([wq, zeros(depth, r, MLA_HEADS, HEAD_PAD - QK_DIM)], axis=-1)
    q_rot = jnp.concatenate([zeros(depth, r, MLA_HEADS, MLA_NOPE), _rotate_cols(wq[..., MLA_NOPE:]),
                             zeros(depth, r, MLA_HEADS, HEAD_PAD - QK_DIM)], axis=-1)
    wq2 = jnp.concatenate([q_main.reshape(depth, r, -1), q_rot.reshape(depth, r, -1)], axis=-1).astype(BF16)

    rk = mla_wkv_b.shape[1]
    wkv = mla_wkv_b.reshape(depth, rk, MLA_HEADS, MLA_NOPE + MLA_V)
    k_part = jnp.concatenate([wkv[..., :MLA_NOPE], zeros(depth, rk, MLA_HEADS, HEAD_PAD - MLA_NOPE)], axis=-1)
    wk = k_part.reshape(depth, rk, -1).astype(BF16)
    v_part = jnp.concatenate([wkv[..., MLA_NOPE:], zeros(depth, rk, MLA_HEADS, HEAD_PAD - MLA_V)], axis=-1)
    wvt = jnp.swapaxes(v_part.reshape(depth, rk, -1), 1, 2).astype(BF16)

    gr = GLA_GATE_RANK
    wgate = jnp.zeros((depth, LANES, 2 * GLA_K), F32)
    wgate = wgate.at[:, 0:gr, 0:GLA_K].set(gla_w_gate[:, 0]).at[:, gr:2 * gr, GLA_K:].set(gla_w_gate[:, 1])
    bgate = gla_b_gate.reshape(depth, 1, 2 * GLA_K)
    return wa, wq2, wk, wvt, wgate.astype(BF16), bgate


def _bucket_tables(counts, n_sorted_tiles):
    padded = ((counts + TM - 1) // TM) * TM
    ends = jnp.cumsum(padded)
    offsets = ends - padded
    n_used = (ends[-1] // TM).astype(jnp.int32)
    tile_start = jnp.arange(n_sorted_tiles, dtype=jnp.int32) * TM
    tile_bucket = jnp.sum((tile_start[:, None] >= ends[None, :]).astype(jnp.int32), axis=1)
    last_bucket = jnp.sum((jnp.maximum(ends[-1] - TM, 0) >= ends).astype(jnp.int32))
    tile_bucket = jnp.where(tile_start < ends[-1], tile_bucket, last_bucket)
    pair_lo = np.array([a for a in range(PER_GROUP) for b in range(a + 1, PER_GROUP)], np.int32)
    pair_hi = np.array([b for a in range(PER_GROUP) for b in range(a + 1, PER_GROUP)], np.int32)
    group = tile_bucket // N_PAIRS
    pair = tile_bucket % N_PAIRS
    e1 = group * PER_GROUP + jnp.asarray(pair_lo)[pair]
    e2 = group * PER_GROUP + jnp.asarray(pair_hi)[pair]
    return offsets.astype(jnp.int32), e1.astype(jnp.int32), e2.astype(jnp.int32), n_used.reshape(1)


def kernel(x, c, ctx, c_ctx, w_mod, b_mod, w_in, mla_q_norm, mla_wq_b, mla_kv_norm, mla_wkv_b, mla_w_o, conv_w,
           conv_b, conv_w_o, gla_w_gate, gla_b_gate, gla_norm, gla_w_o, w_out, ln1_g, ln1_b, router_w, router_b,
           exp_wg, exp_wu, exp_wd, ln2_g, ln2_b):
    bn, seq_len, d = x.shape
    ctx_len = ctx.shape[1]
    depth = w_in.shape[0]
    assert ctx_len == TM and seq_len % TM == 0 and d == D_MODEL and bn < 8
    alpha = (2 * depth) ** 0.25
    t = ctx_len + seq_len
    n_tok = bn * t
    n_sorted_tiles = n_tok // TM + N_BUCKETS

    cond = jnp.concatenate([c, c_ctx[None, :], jnp.zeros((7 - bn, d), F32)], axis=0)
    mods = _modulation(cond, w_mod, b_mod).reshape(depth, 8, 6, d)
    rope_tab = _rope_tables(ctx_len, seq_len)
    wa, wq2, wk, wvt, wgate, bgate = _prep_layer_weights(w_in, mla_wq_b, mla_wkv_b, gla_w_gate, gla_b_gate)
    row = lambda a: a.reshape(depth, 1, a.shape[-1])
    gq, gkv, cbias, gnorm = row(mla_q_norm), row(mla_kv_norm), row(conv_b), row(gla_norm)
    lng1, lnb1, lng2, lnb2 = row(ln1_g), row(ln1_b), row(ln2_g), row(ln2_b)
    wmla, wconv, wgla, wout = (a.astype(BF16) for a in (mla_w_o, conv_w_o, gla_w_o, w_out))
    rw = jnp.concatenate([router_w, jnp.zeros((d, LANES - N_EXPERTS), F32)], axis=1)
    rw_hi = rw.astype(BF16)
    rw_lo = (rw - rw_hi.astype(F32)).astype(BF16)
    rb = router_b.reshape(N_EXPERTS, 1)
    de = exp_wg.shape[-1]
    wg = exp_wg.astype(BF16).reshape(depth * N_EXPERTS, d, de)
    wu = exp_wu.astype(BF16).reshape(depth * N_EXPERTS, d, de)
    wd = exp_wd.astype(BF16).reshape(depth * N_EXPERTS, de, d)

    xt = jnp.concatenate([ctx, x], axis=1)
    for l in range(depth):
        q, k, vt, y, cb, gl, dec, sr, sg = _in_proj(xt, mods, l, rope_tab, wa[l], gq[l], wq2[l], gkv[l], wk[l],
                                                    wvt[l], wgate[l], bgate[l])
        att = _attention(q, k, vt)
        o_f, o_b = _gla(gl, dec)
        x1, u_tok, br, counts = _finish(xt, mods, l, y, cb, sr, sg, att, o_f, o_b, wmla[l], conv_w[l], cbias[l],
                                        wconv[l], gnorm[l], wgla[l], wout[l], lng1[l], lnb1[l], rw_hi, rw_lo, rb, alpha)
        counts = counts[:N_BUCKETS, 0]
        offsets, tile_e1, tile_e2, n_used = _bucket_tables(counts, n_sorted_tiles)
        br = br.reshape(bn * (t // TM), 8, TM)
        pos = _place(offsets, br[:, 0, :], br[:, 1, :])
        src = _invert(pos, offsets, counts, n_used, n_sorted_tiles)
        ys = _experts(tile_e1, tile_e2, n_used, src, u_tok, wg, wu, wd, l)
        xt = _ffn_out(pos, x1, mods, l, ys, lng2[l], lnb2[l], alpha, first_tile=int(l == depth - 1))
    return xt
```

```python
import functools

import numpy as np
import jax
import jax.numpy as jnp
from jax import lax
from jax.experimental import pallas as pl
from jax.experimental.pallas import tpu as pltpu

D_MODEL = 1024
GRID_W = 64
MLA_HEADS = 8
MLA_NOPE = 64
MLA_ROPE = 32
MLA_V = 64
MLA_Q_RANK = 256
MLA_KV_RANK = 128
ROPE_BASE = 10000.0
CONV_CH = 256
GLA_HEADS = 4
GLA_DK = 32
GLA_DV = 64
GLA_GATE_RANK = 16
GLA_TAU = 16.0
GLA_CHUNK = 64
N_EXPERTS = 16
N_GROUPS = 4
PER_GROUP = N_EXPERTS // N_GROUPS
D_EXPERT = 512
NORM_EPS = 1e-6
F32 = jnp.float32
BF16 = jnp.bfloat16

LANES = 128
TM = 256
HEAD_PAD = 128
QK_DIM = MLA_NOPE + MLA_ROPE
GLA_K = GLA_HEADS * GLA_DK
GLA_V = GLA_HEADS * GLA_DV
N_PAIRS = PER_GROUP * (PER_GROUP - 1) // 2
N_BUCKETS = N_GROUPS * N_PAIRS
BUCKET_ROWS = 32
TOKEN_ROWS = D_MODEL // LANES
ISSUE_UNROLL = 8
ATTN_KEY_CHUNK = 256
VMEM_LIMIT = 56 * 1024 * 1024

C_CQ = 0
C_CKV = C_CQ + MLA_Q_RANK
C_KPE = C_CKV + MLA_KV_RANK
C_CONV = C_KPE + 2 * LANES
C_GLA = C_CONV + 3 * CONV_CH
C_GATES = C_GLA + 2 * GLA_K + 2 * GLA_V + LANES
W_A_COLS = C_GATES + 3 * D_MODEL


def _params(*sem):
    return pltpu.CompilerParams(dimension_semantics=sem, vmem_limit_bytes=VMEM_LIMIT)


def _const_spec(shape):
    n = len(shape)
    return pl.BlockSpec(shape, lambda *_: (0,) * n, pipeline_mode=pl.Buffered(1))


def _dot(a, b):
    return jnp.dot(a, b, preferred_element_type=F32)


def _dot_nt(a, b):
    return lax.dot_general(a, b, (((1,), (1,)), ((), ())), preferred_element_type=F32)


def _dot_tn(a, b):
    return lax.dot_general(a, b, (((0,), (0,)), ((), ())), preferred_element_type=F32)


def _split_hi_lo(x):
    hi = x.astype(BF16)
    lo = (x - hi.astype(F32)).astype(BF16)
    return hi, lo


def _sigmoid(x):
    return 1.0 / (1.0 + jnp.exp(-x))


def _silu(x):
    return x * _sigmoid(x)


def _layer_norm(v, g, b):
    mu = jnp.mean(v, axis=-1, keepdims=True)
    d = v - mu
    var = jnp.mean(d * d, axis=-1, keepdims=True)
    return d * lax.rsqrt(var + NORM_EPS) * g + b


def _rms(v, g):
    return v * lax.rsqrt(jnp.mean(v * v, axis=-1, keepdims=True) + NORM_EPS) * g


def _mod_kernel(c_ref, w_ref, b_ref, o_ref):
    o_ref[...] = _dot(_silu(c_ref[...]).astype(BF16), w_ref[...].astype(BF16)) + b_ref[...]


def _modulation(cond, w_mod, b_mod):
    depth, d, n = w_mod.shape
    tn = n // 4
    return pl.pallas_call(
        _mod_kernel,
        grid=(depth, n // tn),
        in_specs=[pl.BlockSpec((8, d), lambda l, j: (0, 0)),
                  pl.BlockSpec((None, d, tn), lambda l, j: (l, 0, j)),
                  pl.BlockSpec((None, 1, tn), lambda l, j: (l, 0, j))],
        out_specs=pl.BlockSpec((None, 8, tn), lambda l, j: (l, 0, j)),
        out_shape=jax.ShapeDtypeStruct((depth, 8, n), F32),
        compiler_params=_params("parallel", "parallel"),
        name="modulation",
    )(cond, w_mod, b_mod.reshape(depth, 1, n))


def _in_proj_kernel(x_ref, mod_ref, rope_ref, wa_ref, gq_ref, wq_ref, gkv_ref, wk_ref, wvt_ref, wgate_ref, bgate_ref,
                    q_ref, k_ref, vt_ref, y_ref, cb_ref, gl_ref, dec_ref, sr_ref, sg_ref):
    u = (x_ref[...] * (1.0 + mod_ref[1:2, :]) + mod_ref[0:1, :]).astype(BF16)
    cq_tab, sq_tab = rope_ref[:, 0:LANES], rope_ref[:, LANES:2 * LANES]
    ck_tab, sk_tab = rope_ref[:, 2 * LANES:3 * LANES], rope_ref[:, 3 * LANES:4 * LANES]

    cq = _dot(u, wa_ref[:, C_CQ:C_CKV])
    q2 = _dot(_rms(cq, gq_ref[...]).astype(BF16), wq_ref[...])
    nq = MLA_HEADS * HEAD_PAD
    for h in range(MLA_HEADS):
        a, b = h * HEAD_PAD, (h + 1) * HEAD_PAD
        q_ref[:, a:b] = (q2[:, a:b] * cq_tab + q2[:, nq + a:nq + b] * sq_tab).astype(BF16)

    ckv = _dot(u, wa_ref[:, C_CKV:C_KPE])
    ckvn = _rms(ckv, gkv_ref[...]).astype(BF16)
    kn = _dot(ckvn, wk_ref[...])
    kpe2 = _dot(u, wa_ref[:, C_KPE:C_CONV])
    kpe = kpe2[:, 0:LANES] * ck_tab + kpe2[:, LANES:2 * LANES] * sk_tab
    for h in range(MLA_HEADS):
        a, b = h * HEAD_PAD, (h + 1) * HEAD_PAD
        k_ref[:, a:b] = (kn[:, a:b] + kpe).astype(BF16)
    vt = _dot_nt(wvt_ref[...], ckvn)
    vrow = lax.broadcasted_iota(jnp.int32, vt.shape, 0) % HEAD_PAD
    vt_ref[...] = jnp.where(vrow == MLA_V, 1.0, vt).astype(BF16)

    cv = _dot(u, wa_ref[:, C_CONV:C_GLA])
    cb_ref[...] = cv[:, 0:CONV_CH]
    y_ref[...] = cv[:, CONV_CH:2 * CONV_CH] * cv[:, 2 * CONV_CH:3 * CONV_CH]

    g = _dot(u, wa_ref[:, C_GLA:C_GATES])
    gq = g[:, 0:GLA_K] * (GLA_DK ** -0.5)
    gk = g[:, GLA_K:2 * GLA_K]
    gv = g[:, 2 * GLA_K:2 * GLA_K + GLA_V]
    gr = g[:, 2 * GLA_K + GLA_V:2 * GLA_K + 2 * GLA_V]
    low = g[:, 2 * GLA_K + 2 * GLA_V:]
    pre = _dot(low.astype(BF16), wgate_ref[...]) + bgate_ref[...]
    logg = (jnp.minimum(pre, 0.0) - jnp.log(1.0 + jnp.exp(-jnp.abs(pre)))) * (1.0 / GLA_TAU)
    lf, lb = logg[:, 0:GLA_K], logg[:, GLA_K:2 * GLA_K]
    tm = lf.shape[0]
    ri = lax.broadcasted_iota(jnp.int32, (tm, tm), 0)
    ci = lax.broadcasted_iota(jnp.int32, (tm, tm), 1)
    same = (ri // GLA_CHUNK) == (ci // GLA_CHUNK)
    m_low = jnp.where(same & (ci <= ri), 1.0, 0.0).astype(BF16)
    m_up = jnp.where(same & (ci >= ri), 1.0, 0.0).astype(BF16)
    pieces = jnp.concatenate(_split_hi_lo(lf) + _split_hi_lo(lb), axis=1)
    pm = _dot(m_low, pieces)
    pu = _dot(m_up, pieces)
    pre_f = pm[:, 0:GLA_K] + pm[:, GLA_K:2 * GLA_K]
    pre_b = pm[:, 2 * GLA_K:3 * GLA_K] + pm[:, 3 * GLA_K:4 * GLA_K]
    suf_f = pu[:, 0:GLA_K] + pu[:, GLA_K:2 * GLA_K]
    suf_b = pu[:, 2 * GLA_K:3 * GLA_K] + pu[:, 3 * GLA_K:4 * GLA_K]
    gl_ref[:, 0:GLA_K] = (gq * jnp.exp(pre_f)).astype(BF16)
    gl_ref[:, GLA_K:2 * GLA_K] = (gk * jnp.exp(-pre_f)).astype(BF16)
    gl_ref[:, 2 * GLA_K:3 * GLA_K] = (gk * jnp.exp(suf_f - lf)).astype(BF16)
    gl_ref[:, 3 * GLA_K:4 * GLA_K] = (gq * jnp.exp(suf_b)).astype(BF16)
    gl_ref[:, 4 * GLA_K:5 * GLA_K] = (gk * jnp.exp(-suf_b)).astype(BF16)
    gl_ref[:, 5 * GLA_K:6 * GLA_K] = (gk * jnp.exp(pre_b - lb)).astype(BF16)
    gl_ref[:, 6 * GLA_K:6 * GLA_K + GLA_V] = gv.astype(BF16)
    dec_ref[:, 0:GLA_K] = jnp.exp(pre_f + suf_f - lf)
    dec_ref[:, GLA_K:2 * GLA_K] = jnp.exp(pre_b + suf_b - lb)
    sr_ref[...] = _silu(gr)

    for j in range(3):
        a, b = j * D_MODEL, (j + 1) * D_MODEL
        sg_ref[:, a:b] = _sigmoid(_dot(u, wa_ref[:, C_GATES + a:C_GATES + b])).astype(BF16)


def _in_proj(x, mods, l, rope_tab, wa, gq, wq, gkv, wk, wvt, wgate, bgate):
    bn, t, d = x.shape
    nt = t // TM
    tok = lambda w: pl.BlockSpec((None, TM, w), lambda b, i: (b, i, 0))
    outs = [(MLA_HEADS * HEAD_PAD, BF16), (MLA_HEADS * HEAD_PAD, BF16), None,
            (CONV_CH, F32), (CONV_CH, F32), (6 * GLA_K + GLA_V, BF16), (2 * GLA_K, F32), (GLA_V, F32),
            (3 * D_MODEL, BF16)]
    vt_rows = MLA_HEADS * HEAD_PAD
    vt_spec = pl.BlockSpec((None, vt_rows, TM), lambda b, i: (b, 0, i))
    vt_shape = jax.ShapeDtypeStruct((bn, vt_rows, t), BF16)
    consts = [wa, gq, wq, gkv, wk, wvt, wgate, bgate]
    return pl.pallas_call(
        _in_proj_kernel,
        grid=(bn, nt),
        in_specs=[tok(d),
                  pl.BlockSpec((None, None, 6, d), lambda b, i: (l, jnp.where(i == 0, bn, b), 0, 0)),
                  pl.BlockSpec((TM, 4 * LANES), lambda b, i: (i, 0))]
                 + [_const_spec(a.shape) for a in consts],
        out_specs=[vt_spec if o is None else tok(o[0]) for o in outs],
        out_shape=[vt_shape if o is None else jax.ShapeDtypeStruct((bn, t, o[0]), o[1]) for o in outs],
        compiler_params=_params("parallel", "parallel"),
        name="in_proj",
    )(x, mods, rope_tab, *consts)


def _attn_kernel(q_ref, k_ref, vt_ref, o_ref, s0_ref, s1_ref, *, ctx_len):
    x_refs = (s0_ref, s1_ref)
    kc = ATTN_KEY_CHUNK

    def attend(nk):
        nc = nk // kc
        accs = []
        for j in range(2):
            q = q_ref[:, j * HEAD_PAD:(j + 1) * HEAD_PAD]
            m_run, m_at = None, []
            for c in range(nc):
                st = _dot_nt(k_ref[c * kc:(c + 1) * kc, j * HEAD_PAD:(j + 1) * HEAD_PAD], q)
                mc = jnp.max(st, axis=0, keepdims=True)
                m_run = mc if m_run is None else jnp.maximum(m_run, mc)
                x_refs[j][c * kc:(c + 1) * kc, :] = (st - m_run).astype(BF16)
                m_at.append(m_run)
            acc = None
            for c in range(nc):
                p = jnp.exp2(x_refs[j][c * kc:(c + 1) * kc, :])
                part = _dot(vt_ref[j * HEAD_PAD:(j + 1) * HEAD_PAD, c * kc:(c + 1) * kc], p)
                if c + 1 < nc:
                    part = part * jnp.exp2(m_at[c] - m_run)
                acc = part if acc is None else acc + part
            accs.append(acc)
        outs = [a[0:MLA_V, :] / a[MLA_V:MLA_V + 1, :] for a in accs]
        o_ref[...] = jnp.concatenate(outs, axis=0).T.astype(o_ref.dtype)

    is_ctx = pl.program_id(2) == 0
    pl.when(is_ctx)(lambda: attend(ctx_len))
    pl.when(jnp.logical_not(is_ctx))(lambda: attend(k_ref.shape[0]))


def _attention(q, k, vt):
    bn, t, _ = q.shape
    hp = MLA_HEADS // 2
    return pl.pallas_call(
        functools.partial(_attn_kernel, ctx_len=TM),
        grid=(bn, hp, t // TM),
        in_specs=[pl.BlockSpec((None, TM, 2 * HEAD_PAD), lambda b, h, i: (b, i, h)),
                  pl.BlockSpec((None, t, 2 * HEAD_PAD), lambda b, h, i: (b, 0, h)),
                  pl.BlockSpec((None, 2 * HEAD_PAD, t), lambda b, h, i: (b, h, 0))],
        out_specs=pl.BlockSpec((None, TM, 2 * MLA_V), lambda b, h, i: (b, i, h)),
        out_shape=jax.ShapeDtypeStruct((bn, t, MLA_HEADS * MLA_V), BF16),
        scratch_shapes=[pltpu.VMEM((t, TM), BF16)] * 2,
        compiler_params=_params("parallel", "parallel", "arbitrary"),
        name="attention",
    )(q, k, vt)


def _gla_kernel(qf_ref, vf_ref, df_ref, qb_ref, vb_ref, db_ref, of_ref, ob_ref, sf_ref, sb_ref):
    @pl.when(pl.program_id(1) == 0)
    def _():
        sf_ref[...] = jnp.zeros_like(sf_ref)
        sb_ref[...] = jnp.zeros_like(sb_ref)

    c = GLA_CHUNK
    lane_k = lax.broadcasted_iota(jnp.int32, (c, GLA_K), 1) // GLA_DK
    lane_v = lax.broadcasted_iota(jnp.int32, (c, GLA_V), 1) // GLA_DV
    ai = lax.broadcasted_iota(jnp.int32, (c, GLA_HEADS * c), 0)
    aj = lax.broadcasted_iota(jnp.int32, (c, GLA_HEADS * c), 1) % c
    state_mask = (lax.broadcasted_iota(jnp.int32, (GLA_V, GLA_K), 0) // GLA_DV ==
                  lax.broadcasted_iota(jnp.int32, (GLA_V, GLA_K), 1) // GLA_DK)
    zero = jnp.zeros((), BF16)

    def chunk(g_ref, v_ref, d_ref, o_ref, s_ref, b, r, causal):
        rows = slice(r * c, (r + 1) * c)
        qd = g_ref[b, rows, 0:GLA_K]
        kd = g_ref[b, rows, GLA_K:2 * GLA_K]
        kend = g_ref[b, rows, 2 * GLA_K:3 * GLA_K]
        v = v_ref[b, rows, :]
        dec = d_ref[b, r * c:r * c + 1, :]
        kd_heads = jnp.concatenate([jnp.where(lane_k == h, kd, zero) for h in range(GLA_HEADS)], axis=0)
        v_heads = jnp.concatenate([jnp.where(lane_v == h, v, zero) for h in range(GLA_HEADS)], axis=0)
        att = jnp.where(causal, _dot_nt(qd, kd_heads), 0.0)
        st = s_ref[b]
        o_ref[b, rows, :] = _dot(att.astype(BF16), v_heads) + _dot_nt(qd, st.astype(BF16))
        s_ref[b] = st * dec + jnp.where(state_mask, _dot_tn(v, kend), 0.0)

    n_chunks = qf_ref.shape[1] // c
    for r in range(n_chunks):
        for b in range(qf_ref.shape[0]):
            chunk(qf_ref, vf_ref, df_ref, of_ref, sf_ref, b, r, aj <= ai)
            chunk(qb_ref, vb_ref, db_ref, ob_ref, sb_ref, b, n_chunks - 1 - r, aj >= ai)


def _gla(gl, dec):
    bn, t, _ = gl.shape
    nt = t // TM
    gb = 2 if bn % 2 == 0 else 1
    fwd = lambda i: i
    bwd = lambda i: jnp.where(i == 0, 0, nt - i)
    spec = lambda w, order, col: pl.BlockSpec((gb, TM, w), lambda b, i: (b, order(i), col))
    return pl.pallas_call(
        _gla_kernel,
        grid=(bn // gb, nt),
        in_specs=[spec(3 * GLA_K, fwd, 0), spec(GLA_V, fwd, 3), spec(GLA_K, fwd, 0),
                  spec(3 * GLA_K, bwd, 1), spec(GLA_V, bwd, 3), spec(GLA_K, bwd, 1)],
        out_specs=[spec(GLA_V, fwd, 0), spec(GLA_V, bwd, 0)],
        out_shape=[jax.ShapeDtypeStruct((bn, t, GLA_V), F32)] * 2,
        scratch_shapes=[pltpu.VMEM((gb, GLA_V, GLA_K), F32)] * 2,
        compiler_params=_params("parallel", "arbitrary"),
        name="gla",
    )(gl, gl, dec, gl, gl, dec)


def _finish_kernel(x_ref, mod_ref, y_ref, yp_ref, yn_ref, cb_ref, sr_ref, sg_ref, att_ref, of_ref, ob_ref,
                   wmla_ref, cw_ref, cbias_ref, wconv_ref, gnorm_ref, wgla_ref, wout_ref, lng_ref, lnb_ref,
                   rwh_ref, rwl_ref, rb_ref, o_ref, u_ref, br_ref, cnt_ref, base_ref, *, alpha):
    i = pl.program_id(1)
    nt = pl.num_programs(1)
    tm = y_ref.shape[0]
    y = y_ref[...]
    has_prev = jnp.logical_and(i != 0, i != 1)
    has_next = jnp.logical_and(i != 0, i != nt - 1)
    prev_row = jnp.where(has_prev, yp_ref[7:8, :], 0.0)
    next_row = jnp.where(has_next, yn_ref[0:1, :], 0.0)
    row = lax.broadcasted_iota(jnp.int32, y.shape, 0)
    y_m1 = jnp.where(row == 0, prev_row, pltpu.roll(y, 1, axis=0))
    y_p1 = jnp.where(row == tm - 1, next_row, pltpu.roll(y, tm - 1, axis=0))
    conv = cbias_ref[...] + y_m1 * cw_ref[0:1, :] + y * cw_ref[1:2, :] + y_p1 * cw_ref[2:3, :]
    conv = cb_ref[...] * conv

    o = of_ref[...] + ob_ref[...]
    gi = lax.broadcasted_iota(jnp.int32, (GLA_V, GLA_V), 0) // GLA_DV
    gj = lax.broadcasted_iota(jnp.int32, (GLA_V, GLA_V), 1) // GLA_DV
    grp = jnp.where(gi == gj, 1.0 / GLA_DV, 0.0).astype(BF16)
    hi, lo = _split_hi_lo(o * o)
    ms = _dot(hi, grp) + _dot(lo, grp)
    gla = o * lax.rsqrt(ms + NORM_EPS) * gnorm_ref[...] * sr_ref[...]

    d = x_ref.shape[1]
    merged = (sg_ref[:, 0:d].astype(F32) * _dot(att_ref[...], wmla_ref[...])
              + sg_ref[:, d:2 * d].astype(F32) * _dot(conv.astype(BF16), wconv_ref[...])
              + sg_ref[:, 2 * d:3 * d].astype(F32) * _dot(gla.astype(BF16), wgla_ref[...]))
    m = _dot(merged.astype(BF16), wout_ref[...])
    x1 = _layer_norm(alpha * x_ref[...] + mod_ref[2:3, :] * m, lng_ref[...], lnb_ref[...])
    o_ref[...] = x1
    _route_tile(x1, mod_ref, rwh_ref, rwl_ref, rb_ref, u_ref, br_ref, cnt_ref, base_ref)


def _finish(x, mods, l, y, cb, sr, sg, att, o_f, o_b, wmla, cw, cbias, wconv, gnorm, wgla, wout, lng, lnb, rw_hi,
            rw_lo, rb, alpha):
    bn, t, d = x.shape
    nt = t // TM
    tok = lambda w: pl.BlockSpec((None, TM, w), lambda b, i: (b, i, 0))
    rows8 = TM // 8
    consts = [wmla, cw, cbias, wconv, gnorm, wgla, wout, lng, lnb, rw_hi, rw_lo, rb]
    return pl.pallas_call(
        functools.partial(_finish_kernel, alpha=alpha),
        grid=(bn, nt),
        in_specs=[tok(d),
                  pl.BlockSpec((None, None, 6, d), lambda b, i: (l, jnp.where(i == 0, bn, b), 0, 0)),
                  tok(CONV_CH),
                  pl.BlockSpec((None, 8, CONV_CH), lambda b, i: (b, jnp.maximum(i * rows8 - 1, 0), 0)),
                  pl.BlockSpec((None, 8, CONV_CH), lambda b, i: (b, jnp.minimum((i + 1) * rows8, nt * rows8 - 1), 0)),
                  tok(CONV_CH), tok(GLA_V), tok(3 * d), tok(MLA_HEADS * MLA_V), tok(GLA_V), tok(GLA_V)]
                 + [_const_spec(a.shape) for a in consts],
        out_specs=[tok(d),
                   pl.BlockSpec((TM * TOKEN_ROWS, LANES), lambda b, i: (b * nt + i, 0)),
                   pl.BlockSpec((None, None, 8, TM), lambda b, i: (b, i, 0, 0)),
                   pl.BlockSpec((BUCKET_ROWS, LANES), lambda b, i: (0, 0))],
        out_shape=[jax.ShapeDtypeStruct((bn, t, d), F32),
                   jax.ShapeDtypeStruct((bn * t * TOKEN_ROWS, LANES), F32),
                   jax.ShapeDtypeStruct((bn, nt, 8, TM), jnp.int32),
                   jax.ShapeDtypeStruct((BUCKET_ROWS, LANES), jnp.int32)],
        scratch_shapes=[pltpu.VMEM((BUCKET_ROWS, LANES), F32)],
        compiler_params=_params("arbitrary", "arbitrary"),
        name="finish",
    )(x, mods, y, y, y, cb, sr, sg, att, o_f, o_b, *consts)


def _route_tile(x1, mod_ref, rwh_ref, rwl_ref, rb_ref, u_ref, br_ref, cnt_ref, base_ref):
    first = jnp.logical_and(pl.program_id(0) == 0, pl.program_id(1) == 0)

    @pl.when(first)
    def _():
        base_ref[...] = jnp.zeros_like(base_ref)

    tm = x1.shape[0]
    u = x1 * (1.0 + mod_ref[4:5, :]) + mod_ref[3:4, :]
    hi, lo = _split_hi_lo(u)
    for j in range(TOKEN_ROWS):
        u_ref[pl.ds(j, tm, stride=TOKEN_ROWS), :] = u[:, j * LANES:(j + 1) * LANES]
    logits = _dot(hi, rwh_ref[...]) + _dot(lo, rwh_ref[...]) + _dot(hi, rwl_ref[...])
    s = _sigmoid(logits.T[0:N_EXPERTS, :])
    sel = s + rb_ref[...]

    sel_e = [sel[e:e + 1, :] for e in range(N_EXPERTS)]
    chosen, gscore = [], []
    for g in range(N_GROUPS):
        members = list(range(g * PER_GROUP, (g + 1) * PER_GROUP))
        picked = []
        for a in members:
            beaten = jnp.zeros((1, tm), F32)
            for b in members:
                if b != a:
                    wins = (sel_e[b] >= sel_e[a]) if b < a else (sel_e[b] > sel_e[a])
                    beaten = beaten + jnp.where(wins, 1.0, 0.0)
            picked.append(beaten < 2.0)
        chosen.append(picked)
        score = jnp.zeros((1, tm), F32)
        for a, p in zip(members, picked):
            score = score + jnp.where(p, sel_e[a], 0.0)
        gscore.append(score)
    best = []
    for g in range(N_GROUPS):
        ok = jnp.ones((1, tm), jnp.bool_)
        for h in range(N_GROUPS):
            if h < g:
                ok = jnp.logical_and(ok, gscore[g] > gscore[h])
            elif h > g:
                ok = jnp.logical_and(ok, gscore[g] >= gscore[h])
        best.append(ok)

    bucket_rows = []
    for g in range(N_GROUPS):
        taken = [jnp.logical_and(best[g], p) for p in chosen[g]]
        for a in range(PER_GROUP):
            for b in range(a + 1, PER_GROUP):
                bucket_rows.append(jnp.where(jnp.logical_and(taken[a], taken[b]), 1.0, 0.0))

    onehot = jnp.concatenate(bucket_rows + [jnp.zeros((BUCKET_ROWS - N_BUCKETS, tm), F32)], axis=0)
    ri = lax.broadcasted_iota(jnp.int32, (tm, tm), 0)
    ci = lax.broadcasted_iota(jnp.int32, (tm, tm), 1)
    before = _dot(onehot.astype(BF16), jnp.where(ri < ci, 1.0, 0.0).astype(BF16))
    base = base_ref[:, 0:1]
    rank = jnp.sum(onehot * (before + base), axis=0, keepdims=True)
    bidx = lax.broadcasted_iota(jnp.int32, onehot.shape, 0).astype(F32)
    bucket = jnp.sum(onehot * bidx, axis=0, keepdims=True)
    br_ref[...] = jnp.concatenate([bucket, rank, jnp.zeros((6, tm), F32)], axis=0).astype(jnp.int32)
    new_base = base + jnp.sum(onehot, axis=1, keepdims=True)
    base_ref[...] = jnp.broadcast_to(new_base, base_ref.shape)
    cnt_ref[...] = jnp.broadcast_to(new_base, cnt_ref.shape).astype(jnp.int32)


def _place_kernel(off_ref, bucket_ref, rank_ref, pos_ref):
    bucket = bucket_ref[...]
    pos = rank_ref[...]
    for b in range(N_BUCKETS):
        pos = pos + jnp.where(bucket == b, off_ref[b], 0)
    pos_ref[...] = pos


def _place(offsets, bucket, rank):
    return pl.pallas_call(
        _place_kernel,
        in_specs=[pl.BlockSpec(memory_space=pltpu.SMEM), pl.BlockSpec(memory_space=pltpu.VMEM),
                  pl.BlockSpec(memory_space=pltpu.VMEM)],
        out_specs=pl.BlockSpec(memory_space=pltpu.VMEM),
        out_shape=jax.ShapeDtypeStruct(bucket.shape, jnp.int32),
        name="place",
    )(offsets, bucket, rank)


def _slot_copy(src_ref, src_slot, dst_ref, dst_slot, sem, rows):
    return pltpu.make_async_copy(src_ref.at[pl.ds(pl.multiple_of(src_slot * rows, rows), rows)],
                                 dst_ref.at[pl.ds(pl.multiple_of(dst_slot * rows, rows), rows)], sem)


def _invert_kernel(pos_ref, off_ref, cnt_ref, used_ref, src_ref):
    def point_at_zero(p, carry):
        src_ref[p] = 0
        return carry

    for b in range(N_BUCKETS):
        start = off_ref[b] + cnt_ref[b]
        lax.fori_loop(start, (start + (TM - 1)) & (-TM), point_at_zero, 0)
    def unused_tile(i, carry):
        for r in range(TM):
            src_ref[i * TM + r] = 0
        return carry

    lax.fori_loop(used_ref[0], src_ref.shape[0] // TM, unused_tile, 0)

    def one(t, carry):
        src_ref[pos_ref[t]] = t
        return carry

    lax.fori_loop(0, pos_ref.shape[0], one, 0, unroll=ISSUE_UNROLL)


def _invert(pos, offsets, counts, n_used, n_sorted_tiles):
    smem = pl.BlockSpec(memory_space=pltpu.SMEM)
    return pl.pallas_call(
        _invert_kernel,
        in_specs=[smem, smem, smem, smem],
        out_specs=smem,
        out_shape=jax.ShapeDtypeStruct((n_sorted_tiles * TM,), jnp.int32),
        name="invert",
    )(pos.reshape(-1), offsets, counts, n_used).reshape(n_sorted_tiles, TM)


def _expert_kernel(e1_ref, e2_ref, used_ref, src_ref, nxt_ref, u_ref, rw1_ref, rw2_ref, wg1_ref, wu1_ref, wd1_ref,
                   wg2_ref, wu2_ref, wd2_ref, y_ref, buf0_ref, buf1_ref, sem):
    del e1_ref, e2_ref
    i = pl.program_id(0)
    n_used = used_ref[0]
    tm = buf0_ref.shape[0] // TOKEN_ROWS
    bufs = (buf0_ref, buf1_ref)

    @pl.when(i == 0)
    def _():
        def issue(r, carry):
            _slot_copy(u_ref, src_ref[0, r], buf0_ref, r, sem.at[0], TOKEN_ROWS).start()
            return carry
        lax.fori_loop(0, tm, issue, 0, unroll=ISSUE_UNROLL)

    def run(s):
        cur, nxt = bufs[s], bufs[1 - s]

        @pl.when(i < n_used)
        def _():
            for r in range(tm):
                _slot_copy(u_ref, nxt_ref[0, r], nxt, r, sem.at[1 - s], TOKEN_ROWS).start()

        pltpu.make_async_copy(u_ref.at[pl.ds(0, tm * TOKEN_ROWS)], cur, sem.at[s]).wait()

        @pl.when(i < n_used)
        def _():
            u = jnp.concatenate([cur[pl.ds(j, tm, stride=TOKEN_ROWS), :] for j in range(TOKEN_ROWS)], axis=1)
            s1 = _sigmoid(jnp.sum(u * rw1_ref[...], axis=-1, keepdims=True))
            s2 = _sigmoid(jnp.sum(u * rw2_ref[...], axis=-1, keepdims=True))
            x = u.astype(BF16)
            h1 = _silu(_dot(x, wg1_ref[...])) * _dot(x, wu1_ref[...])
            h2 = _silu(_dot(x, wg2_ref[...])) * _dot(x, wu2_ref[...])
            y = ((s1 / (s1 + s2)) * _dot(h1.astype(BF16), wd1_ref[...])
                 + (s2 / (s1 + s2)) * _dot(h2.astype(BF16), wd2_ref[...]))
            for j in range(TOKEN_ROWS):
                y_ref[pl.ds(j, tm, stride=TOKEN_ROWS), :] = y[:, j * LANES:(j + 1) * LANES]

    requested = i <= n_used
    pl.when(jnp.logical_and(requested, i % 2 == 0))(lambda: run(0))
    pl.when(jnp.logical_and(requested, i % 2 == 1))(lambda: run(1))

    @pl.when(i >= n_used)
    def _():
        y_ref[...] = jnp.zeros_like(y_ref)


def _experts(tile_e1, tile_e2, n_used, src, u_tok, rw_rows, wg, wu, wd, l):
    n_tiles = src.shape[0]
    d, de = wg.shape[1], wg.shape[2]
    rw = lambda tbl: pl.BlockSpec((None, 1, d), lambda i, e1, e2, n: (tbl(e1, e2)[i], 0, 0))
    up = lambda tbl: pl.BlockSpec((None, d, de), lambda i, e1, e2, n: (l * N_EXPERTS + tbl(e1, e2)[i], 0, 0))
    down = lambda tbl: pl.BlockSpec((None, de, d), lambda i, e1, e2, n: (l * N_EXPERTS + tbl(e1, e2)[i], 0, 0))
    first = lambda e1, e2: e1
    second = lambda e1, e2: e2
    src3 = src.reshape(n_tiles, 1, TM)
    return pl.pallas_call(
        _expert_kernel,
        grid_spec=pltpu.PrefetchScalarGridSpec(
            num_scalar_prefetch=3,
            grid=(n_tiles,),
            in_specs=[pl.BlockSpec((None, 1, TM), lambda i, e1, e2, n: (i, 0, 0), memory_space=pltpu.SMEM),
                      pl.BlockSpec((None, 1, TM), lambda i, e1, e2, n: (jnp.minimum(i + 1, n_tiles - 1), 0, 0),
                                   memory_space=pltpu.SMEM),
                      pl.BlockSpec(memory_space=pl.ANY), rw(first), rw(second),
                      up(first), up(first), down(first), up(second), up(second), down(second)],
            out_specs=pl.BlockSpec((TM * TOKEN_ROWS, LANES), lambda i, e1, e2, n: (i, 0)),
            scratch_shapes=[pltpu.VMEM((TM * TOKEN_ROWS, LANES), F32), pltpu.VMEM((TM * TOKEN_ROWS, LANES), F32),
                            pltpu.SemaphoreType.DMA((2,))],
        ),
        out_shape=jax.ShapeDtypeStruct((n_tiles * TM * TOKEN_ROWS, LANES), F32),
        compiler_params=_params("arbitrary"),
        name="experts",
    )(tile_e1, tile_e2, n_used, src3, src3, u_tok, rw_rows, rw_rows, wg, wu, wd, wg, wu, wd)


def _ffn_out_kernel(pos_ref, nxt_ref, x_ref, mod_ref, ys_ref, lng_ref, lnb_ref, o_ref, buf_ref, sem, *, alpha):
    tm = x_ref.shape[0]
    step = pl.program_id(0) * pl.num_programs(1) + pl.program_id(1)
    n_steps = pl.num_programs(0) * pl.num_programs(1)
    slot = step % 2

    @pl.when(step == 0)
    def _():
        def issue(r, carry):
            _slot_copy(ys_ref, pos_ref[0, r], buf_ref.at[0], r, sem.at[0], TOKEN_ROWS).start()
            return carry
        lax.fori_loop(0, tm, issue, 0, unroll=ISSUE_UNROLL)

    @pl.when(step + 1 < n_steps)
    def _():
        for r in range(tm):
            _slot_copy(ys_ref, nxt_ref[0, r], buf_ref.at[1 - slot], r, sem.at[1 - slot], TOKEN_ROWS).start()
    cur = buf_ref.at[slot]
    pltpu.make_async_copy(ys_ref.at[pl.ds(0, tm * TOKEN_ROWS)], cur, sem.at[slot]).wait()
    f = jnp.concatenate([cur[pl.ds(j, tm, stride=TOKEN_ROWS), :] for j in range(TOKEN_ROWS)], axis=1)
    o_ref[...] = _layer_norm(alpha * x_ref[...] + mod_ref[5:6, :] * f, lng_ref[...], lnb_ref[...])


def _ffn_out(pos, x1, mods, l, ys, lng, lnb, alpha, first_tile):
    bn, t, d = x1.shape
    nt = t // TM
    ng = nt - first_tile
    pos3 = pos.reshape(bn * nt, 1, TM)

    def next_tile(b, i):
        s = jnp.minimum(b * ng + i + 1, bn * ng - 1)
        return ((s // ng) * nt + s % ng + first_tile, 0, 0)

    return pl.pallas_call(
        functools.partial(_ffn_out_kernel, alpha=alpha),
        grid=(bn, ng),
        in_specs=[pl.BlockSpec((None, 1, TM), lambda b, i: (b * nt + i + first_tile, 0, 0), memory_space=pltpu.SMEM),
                  pl.BlockSpec((None, 1, TM), next_tile, memory_space=pltpu.SMEM),
                  pl.BlockSpec((None, TM, d), lambda b, i: (b, i + first_tile, 0)),
                  pl.BlockSpec((None, None, 6, d), lambda b, i: (l, jnp.where(i + first_tile == 0, bn, b), 0, 0)),
                  pl.BlockSpec(memory_space=pl.ANY),
                  _const_spec(lng.shape), _const_spec(lnb.shape)],
        out_specs=pl.BlockSpec((None, TM, d), lambda b, i: (b, i, 0)),
        out_shape=jax.ShapeDtypeStruct((bn, ng * TM, d), F32),
        scratch_shapes=[pltpu.VMEM((2, TM * TOKEN_ROWS, LANES), F32), pltpu.SemaphoreType.DMA((2,))],
        compiler_params=_params("arbitrary", "arbitrary"),
        name="ffn_out",
    )(pos3, pos3, x1, mods, ys, lng, lnb)


def _rope_tables(ctx_len, seq_len):
    t = np.arange(seq_len)
    half = MLA_ROPE // 2
    inv_freq = 1.0 / (ROPE_BASE ** (np.arange(0, half, 2, dtype=np.float32) / half))
    ang_r = (t // GRID_W).astype(np.float32)[:, None] * inv_freq[None, :]
    ang_c = (t % GRID_W).astype(np.float32)[:, None] * inv_freq[None, :]
    ang = np.concatenate([ang_r, ang_r, ang_c, ang_c], axis=-1).astype(np.float32)
    cos = np.concatenate([np.ones((ctx_len, MLA_ROPE), np.float32), np.cos(ang)], axis=0)
    sin = np.concatenate([np.zeros((ctx_len, MLA_ROPE), np.float32), np.sin(ang)], axis=0)
    n = ctx_len + seq_len
    c_tab = np.zeros((n, LANES), np.float32)
    s_tab = np.zeros((n, LANES), np.float32)
    c_tab[:, :MLA_NOPE] = 1.0
    c_tab[:, MLA_NOPE:QK_DIM] = cos
    s_tab[:, MLA_NOPE:QK_DIM] = sin
    scale = np.float32(QK_DIM ** -0.5 * np.log2(np.e))
    return jnp.asarray(np.concatenate([c_tab * scale, s_tab * scale, c_tab, s_tab], axis=1))


def _rotate_cols(w):
    q = MLA_ROPE // 4
    w1, w2, w3, w4 = (w[..., i * q:(i + 1) * q] for i in range(4))
    return jnp.concatenate([-w2, w1, -w4, w3], axis=-1)


def _prep_layer_weights(w_in, mla_wq_b, mla_wkv_b, gla_w_gate, gla_b_gate):
    depth, d, _ = w_in.shape
    splits = np.cumsum([0, MLA_Q_RANK, MLA_KV_RANK, MLA_ROPE, CONV_CH, CONV_CH, CONV_CH, GLA_K, GLA_K, GLA_V,
                        GLA_V, GLA_GATE_RANK, GLA_GATE_RANK, 3 * D_MODEL])
    seg = [w_in[:, :, splits[i]:splits[i + 1]] for i in range(13)]
    zeros = lambda *s: jnp.zeros(s, F32)
    pad_rope = lambda w: jnp.concatenate([zeros(depth, d, MLA_NOPE), w, zeros(depth, d, LANES - QK_DIM)], axis=-1)
    wa = jnp.concatenate(
        [seg[0], seg[1], pad_rope(seg[2]), pad_rope(_rotate_cols(seg[2])), seg[3], seg[4], seg[5], seg[6], seg[7],
         seg[8], seg[9], seg[10], seg[11], zeros(depth, d, LANES - 2 * GLA_GATE_RANK), seg[12]], axis=-1).astype(BF16)

    r = mla_wq_b.shape[1]
    wq = mla_wq_b.reshape(depth, r, MLA_HEADS, QK_DIM)
    q_main = jnp.concatenate([wq, zeros(depth, r, MLA_HEADS, HEAD_PAD - QK_DIM)], axis=-1)
    q_rot = jnp.concatenate([zeros(depth, r, MLA_HEADS, MLA_NOPE), _rotate_cols(wq[..., MLA_NOPE:]),
                             zeros(depth, r, MLA_HEADS, HEAD_PAD - QK_DIM)], axis=-1)
    wq2 = jnp.concatenate([q_main.reshape(depth, r, -1), q_rot.reshape(depth, r, -1)], axis=-1).astype(BF16)

    rk = mla_wkv_b.shape[1]
    wkv = mla_wkv_b.reshape(depth, rk, MLA_HEADS, MLA_NOPE + MLA_V)
    k_part = jnp.concatenate([wkv[..., :MLA_NOPE], zeros(depth, rk, MLA_HEADS, HEAD_PAD - MLA_NOPE)], axis=-1)
    wk = k_part.reshape(depth, rk, -1).astype(BF16)
    v_part = jnp.concatenate([wkv[..., MLA_NOPE:], zeros(depth, rk, MLA_HEADS, HEAD_PAD - MLA_V)], axis=-1)
    wvt = jnp.swapaxes(v_part.reshape(depth, rk, -1), 1, 2).astype(BF16)

    gr = GLA_GATE_RANK
    wgate = jnp.zeros((depth, LANES, 2 * GLA_K), F32)
    wgate = wgate.at[:, 0:gr, 0:GLA_K].set(gla_w_gate[:, 0]).at[:, gr:2 * gr, GLA_K:].set(gla_w_gate[:, 1])
    bgate = gla_b_gate.reshape(depth, 1, 2 * GLA_K)
    return wa, wq2, wk, wvt, wgate.astype(BF16), bgate


def _bucket_tables(counts, n_sorted_tiles):
    padded = ((counts + TM - 1) // TM) * TM
    ends = jnp.cumsum(padded)
    offsets = ends - padded
    n_used = (ends[-1] // TM).astype(jnp.int32)
    tile_start = jnp.arange(n_sorted_tiles, dtype=jnp.int32) * TM
    tile_bucket = jnp.sum((tile_start[:, None] >= ends[None, :]).astype(jnp.int32), axis=1)
    last_bucket = jnp.sum((jnp.maximum(ends[-1] - TM, 0) >= ends).astype(jnp.int32))
    tile_bucket = jnp.where(tile_start < ends[-1], tile_bucket, last_bucket)
    pair_lo = np.array([a for a in range(PER_GROUP) for b in range(a + 1, PER_GROUP)], np.int32)
    pair_hi = np.array([b for a in range(PER_GROUP) for b in range(a + 1, PER_GROUP)], np.int32)
    group = tile_bucket // N_PAIRS
    pair = tile_bucket % N_PAIRS
    e1 = group * PER_GROUP + jnp.asarray(pair_lo)[pair]
    e2 = group * PER_GROUP + jnp.asarray(pair_hi)[pair]
    return offsets.astype(jnp.int32), e1.astype(jnp.int32), e2.astype(jnp.int32), n_used.reshape(1)


def kernel(x, c, ctx, c_ctx, w_mod, b_mod, w_in, mla_q_norm, mla_wq_b, mla_kv_norm, mla_wkv_b, mla_w_o, conv_w,
           conv_b, conv_w_o, gla_w_gate, gla_b_gate, gla_norm, gla_w_o, w_out, ln1_g, ln1_b, router_w, router_b,
           exp_wg, exp_wu, exp_wd, ln2_g, ln2_b):
    bn, seq_len, d = x.shape
    ctx_len = ctx.shape[1]
    depth = w_in.shape[0]
    assert ctx_len == TM and seq_len % TM == 0 and d == D_MODEL and bn < 8
    alpha = (2 * depth) ** 0.25
    t = ctx_len + seq_len
    n_tok = bn * t
    n_sorted_tiles = n_tok // TM + N_BUCKETS

    cond = jnp.concatenate([c, c_ctx[None, :], jnp.zeros((7 - bn, d), F32)], axis=0)
    mods = _modulation(cond, w_mod, b_mod).reshape(depth, 8, 6, d)
    rope_tab = _rope_tables(ctx_len, seq_len)
    wa, wq2, wk, wvt, wgate, bgate = _prep_layer_weights(w_in, mla_wq_b, mla_wkv_b, gla_w_gate, gla_b_gate)
    row = lambda a: a.reshape(depth, 1, a.shape[-1])
    gq, gkv, cbias, gnorm = row(mla_q_norm), row(mla_kv_norm), row(conv_b), row(gla_norm)
    lng1, lnb1, lng2, lnb2 = row(ln1_g), row(ln1_b), row(ln2_g), row(ln2_b)
    wmla, wconv, wgla, wout = (a.astype(BF16) for a in (mla_w_o, conv_w_o, gla_w_o, w_out))
    rw = jnp.concatenate([router_w, jnp.zeros((d, LANES - N_EXPERTS), F32)], axis=1)
    rw_hi = rw.astype(BF16)
    rw_lo = (rw - rw_hi.astype(F32)).astype(BF16)
    rb = router_b.reshape(N_EXPERTS, 1)
    rw_rows = router_w.T.reshape(N_EXPERTS, 1, d)
    de = exp_wg.shape[-1]
    wg = exp_wg.astype(BF16).reshape(depth * N_EXPERTS, d, de)
    wu = exp_wu.astype(BF16).reshape(depth * N_EXPERTS, d, de)
    wd = exp_wd.astype(BF16).reshape(depth * N_EXPERTS, de, d)

    xt = jnp.concatenate([ctx, x], axis=1)
    for l in range(depth):
        q, k, vt, y, cb, gl, dec, sr, sg = _in_proj(xt, mods, l, rope_tab, wa[l], gq[l], wq2[l], gkv[l], wk[l],
                                                    wvt[l], wgate[l], bgate[l])
        att = _attention(q, k, vt)
        o_f, o_b = _gla(gl, dec)
        x1, u_tok, br, counts = _finish(xt, mods, l, y, cb, sr, sg, att, o_f, o_b, wmla[l], conv_w[l], cbias[l],
                                        wconv[l], gnorm[l], wgla[l], wout[l], lng1[l], lnb1[l], rw_hi, rw_lo, rb, alpha)
        counts = counts[:N_BUCKETS, 0]
        offsets, tile_e1, tile_e2, n_used = _bucket_tables(counts, n_sorted_tiles)
        br = br.reshape(bn * (t // TM), 8, TM)
        pos = _place(offsets, br[:, 0, :], br[:, 1, :])
        src = _invert(pos, offsets, counts, n_used, n_sorted_tiles)
        ys = _experts(tile_e1, tile_e2, n_used, src, u_tok, rw_rows, wg, wu, wd, l)
        xt = _ffn_out(pos, x1, mods, l, ys, lng2[l], lnb2[l], alpha, first_tile=int(l == depth - 1))
    return xt
```

```python
import functools

import numpy as np
import jax
import jax.numpy as jnp
from jax import lax
from jax.experimental import pallas as pl
from jax.experimental.pallas import tpu as pltpu

D_MODEL = 1024
GRID_W = 64
MLA_HEADS = 8
MLA_NOPE = 64
MLA_ROPE = 32
MLA_V = 64
MLA_Q_RANK = 256
MLA_KV_RANK = 128
ROPE_BASE = 10000.0
CONV_CH = 256
GLA_HEADS = 4
GLA_DK = 32
GLA_DV = 64
GLA_GATE_RANK = 16
GLA_TAU = 16.0
GLA_CHUNK = 64
N_EXPERTS = 16
N_GROUPS = 4
PER_GROUP = N_EXPERTS // N_GROUPS
D_EXPERT = 512
NORM_EPS = 1e-6
F32 = jnp.float32
BF16 = jnp.bfloat16

LANES = 128
TM = 256
HEAD_PAD = 128
QK_DIM = MLA_NOPE + MLA_ROPE
GLA_K = GLA_HEADS * GLA_DK
GLA_V = GLA_HEADS * GLA_DV
N_PAIRS = PER_GROUP * (PER_GROUP - 1) // 2
N_BUCKETS = N_GROUPS * N_PAIRS
BUCKET_ROWS = 32
TOKEN_ROWS = D_MODEL // LANES
ISSUE_UNROLL = 8
ATTN_KEY_CHUNK = 256
VMEM_LIMIT = 56 * 1024 * 1024

C_CQ = 0
C_CKV = C_CQ + MLA_Q_RANK
C_KPE = C_CKV + MLA_KV_RANK
C_CONV = C_KPE + 2 * LANES
C_GLA = C_CONV + 3 * CONV_CH
C_GATES = C_GLA + 2 * GLA_K + 2 * GLA_V + LANES
W_A_COLS = C_GATES + 3 * D_MODEL


def _params(*sem):
    return pltpu.CompilerParams(dimension_semantics=sem, vmem_limit_bytes=VMEM_LIMIT)


def _const_spec(shape):
    n = len(shape)
    return pl.BlockSpec(shape, lambda *_: (0,) * n, pipeline_mode=pl.Buffered(1))


def _dot(a, b):
    return jnp.dot(a, b, preferred_element_type=F32)


def _dot_nt(a, b):
    return lax.dot_general(a, b, (((1,), (1,)), ((), ())), preferred_element_type=F32)


def _dot_tn(a, b):
    return lax.dot_general(a, b, (((0,), (0,)), ((), ())), preferred_element_type=F32)


def _split_hi_lo(x):
    hi = x.astype(BF16)
    lo = (x - hi.astype(F32)).astype(BF16)
    return hi, lo


def _sigmoid(x):
    return 1.0 / (1.0 + jnp.exp(-x))


def _silu(x):
    return x * _sigmoid(x)


def _layer_norm(v, g, b):
    mu = jnp.mean(v, axis=-1, keepdims=True)
    d = v - mu
    var = jnp.mean(d * d, axis=-1, keepdims=True)
    return d * lax.rsqrt(var + NORM_EPS) * g + b


def _rms(v, g):
    return v * lax.rsqrt(jnp.mean(v * v, axis=-1, keepdims=True) + NORM_EPS) * g


def _mod_kernel(c_ref, w_ref, b_ref, o_ref):
    o_ref[...] = _dot(_silu(c_ref[...]).astype(BF16), w_ref[...].astype(BF16)) + b_ref[...]


def _modulation(cond, w_mod, b_mod):
    depth, d, n = w_mod.shape
    tn = n // 4
    return pl.pallas_call(
        _mod_kernel,
        grid=(depth, n // tn),
        in_specs=[pl.BlockSpec((8, d), lambda l, j: (0, 0)),
                  pl.BlockSpec((None, d, tn), lambda l, j: (l, 0, j)),
                  pl.BlockSpec((None, 1, tn), lambda l, j: (l, 0, j))],
        out_specs=pl.BlockSpec((None, 8, tn), lambda l, j: (l, 0, j)),
        out_shape=jax.ShapeDtypeStruct((depth, 8, n), F32),
        compiler_params=_params("parallel", "parallel"),
        name="modulation",
    )(cond, w_mod, b_mod.reshape(depth, 1, n))


def _in_proj_kernel(x_ref, mod_ref, rope_ref, wa_ref, gq_ref, wq_ref, gkv_ref, wk_ref, wvt_ref, wgate_ref, bgate_ref,
                    q_ref, k_ref, vt_ref, y_ref, cb_ref, gl_ref, dec_ref, sr_ref, sg_ref):
    u = (x_ref[...] * (1.0 + mod_ref[1:2, :]) + mod_ref[0:1, :]).astype(BF16)
    cq_tab, sq_tab = rope_ref[:, 0:LANES], rope_ref[:, LANES:2 * LANES]
    ck_tab, sk_tab = rope_ref[:, 2 * LANES:3 * LANES], rope_ref[:, 3 * LANES:4 * LANES]

    cq = _dot(u, wa_ref[:, C_CQ:C_CKV])
    q2 = _dot(_rms(cq, gq_ref[...]).astype(BF16), wq_ref[...])
    nq = MLA_HEADS * HEAD_PAD
    for h in range(MLA_HEADS):
        a, b = h * HEAD_PAD, (h + 1) * HEAD_PAD
        q_ref[:, a:b] = (q2[:, a:b] * cq_tab + q2[:, nq + a:nq + b] * sq_tab).astype(BF16)

    ckv = _dot(u, wa_ref[:, C_CKV:C_KPE])
    ckvn = _rms(ckv, gkv_ref[...]).astype(BF16)
    kn = _dot(ckvn, wk_ref[...])
    kpe2 = _dot(u, wa_ref[:, C_KPE:C_CONV])
    kpe = kpe2[:, 0:LANES] * ck_tab + kpe2[:, LANES:2 * LANES] * sk_tab
    for h in range(MLA_HEADS):
        a, b = h * HEAD_PAD, (h + 1) * HEAD_PAD
        k_ref[:, a:b] = (kn[:, a:b] + kpe).astype(BF16)
    vt = _dot_nt(wvt_ref[...], ckvn)
    vrow = lax.broadcasted_iota(jnp.int32, vt.shape, 0) % HEAD_PAD
    vt_ref[...] = jnp.where(vrow == MLA_V, 1.0, vt).astype(BF16)

    cv = _dot(u, wa_ref[:, C_CONV:C_GLA])
    cb_ref[...] = cv[:, 0:CONV_CH]
    y_ref[...] = cv[:, CONV_CH:2 * CONV_CH] * cv[:, 2 * CONV_CH:3 * CONV_CH]

    g = _dot(u, wa_ref[:, C_GLA:C_GATES])
    gq = g[:, 0:GLA_K] * (GLA_DK ** -0.5)
    gk = g[:, GLA_K:2 * GLA_K]
    gv = g[:, 2 * GLA_K:2 * GLA_K + GLA_V]
    gr = g[:, 2 * GLA_K + GLA_V:2 * GLA_K + 2 * GLA_V]
    low = g[:, 2 * GLA_K + 2 * GLA_V:]
    pre = _dot(low.astype(BF16), wgate_ref[...]) + bgate_ref[...]
    logg = (jnp.minimum(pre, 0.0) - jnp.log(1.0 + jnp.exp(-jnp.abs(pre)))) * (1.0 / GLA_TAU)
    lf, lb = logg[:, 0:GLA_K], logg[:, GLA_K:2 * GLA_K]
    tm = lf.shape[0]
    ri = lax.broadcasted_iota(jnp.int32, (tm, tm), 0)
    ci = lax.broadcasted_iota(jnp.int32, (tm, tm), 1)
    same = (ri // GLA_CHUNK) == (ci // GLA_CHUNK)
    m_low = jnp.where(same & (ci <= ri), 1.0, 0.0).astype(BF16)
    m_up = jnp.where(same & (ci >= ri), 1.0, 0.0).astype(BF16)
    pieces = jnp.concatenate(_split_hi_lo(lf) + _split_hi_lo(lb), axis=1)
    pm = _dot(m_low, pieces)
    pu = _dot(m_up, pieces)
    pre_f = pm[:, 0:GLA_K] + pm[:, GLA_K:2 * GLA_K]
    pre_b = pm[:, 2 * GLA_K:3 * GLA_K] + pm[:, 3 * GLA_K:4 * GLA_K]
    suf_f = pu[:, 0:GLA_K] + pu[:, GLA_K:2 * GLA_K]
    suf_b = pu[:, 2 * GLA_K:3 * GLA_K] + pu[:, 3 * GLA_K:4 * GLA_K]
    gl_ref[:, 0:GLA_K] = (gq * jnp.exp(pre_f)).astype(BF16)
    gl_ref[:, GLA_K:2 * GLA_K] = (gk * jnp.exp(-pre_f)).astype(BF16)
    gl_ref[:, 2 * GLA_K:3 * GLA_K] = (gk * jnp.exp(suf_f - lf)).astype(BF16)
    gl_ref[:, 3 * GLA_K:4 * GLA_K] = (gq * jnp.exp(suf_b)).astype(BF16)
    gl_ref[:, 4 * GLA_K:5 * GLA_K] = (gk * jnp.exp(-suf_b)).astype(BF16)
    gl_ref[:, 5 * GLA_K:6 * GLA_K] = (gk * jnp.exp(pre_b - lb)).astype(BF16)
    gl_ref[:, 6 * GLA_K:6 * GLA_K + GLA_V] = gv.astype(BF16)
    dec_ref[:, 0:GLA_K] = jnp.exp(pre_f + suf_f - lf)
    dec_ref[:, GLA_K:2 * GLA_K] = jnp.exp(pre_b + suf_b - lb)
    sr_ref[...] = _silu(gr)

    for j in range(3):
        a, b = j * D_MODEL, (j + 1) * D_MODEL
        sg_ref[:, a:b] = _sigmoid(_dot(u, wa_ref[:, C_GATES + a:C_GATES + b])).astype(BF16)


def _in_proj(x, mods, l, rope_tab, wa, gq, wq, gkv, wk, wvt, wgate, bgate):
    bn, t, d = x.shape
    nt = t // TM
    tok = lambda w: pl.BlockSpec((None, TM, w), lambda b, i: (b, i, 0))
    outs = [(MLA_HEADS * HEAD_PAD, BF16), (MLA_HEADS * HEAD_PAD, BF16), None,
            (CONV_CH, F32), (CONV_CH, F32), (6 * GLA_K + GLA_V, BF16), (2 * GLA_K, F32), (GLA_V, F32),
            (3 * D_MODEL, BF16)]
    vt_rows = MLA_HEADS * HEAD_PAD
    vt_spec = pl.BlockSpec((None, vt_rows, TM), lambda b, i: (b, 0, i))
    vt_shape = jax.ShapeDtypeStruct((bn, vt_rows, t), BF16)
    consts = [wa, gq, wq, gkv, wk, wvt, wgate, bgate]
    return pl.pallas_call(
        _in_proj_kernel,
        grid=(bn, nt),
        in_specs=[tok(d),
                  pl.BlockSpec((None, None, 6, d), lambda b, i: (l, jnp.where(i == 0, bn, b), 0, 0)),
                  pl.BlockSpec((TM, 4 * LANES), lambda b, i: (i, 0))]
                 + [_const_spec(a.shape) for a in consts],
        out_specs=[vt_spec if o is None else tok(o[0]) for o in outs],
        out_shape=[vt_shape if o is None else jax.ShapeDtypeStruct((bn, t, o[0]), o[1]) for o in outs],
        compiler_params=_params("parallel", "parallel"),
        name="in_proj",
    )(x, mods, rope_tab, *consts)


def _attn_kernel(q_ref, k_ref, vt_ref, o_ref, s0_ref, s1_ref, *, ctx_len):
    x_refs = (s0_ref, s1_ref)
    kc = ATTN_KEY_CHUNK

    def attend(nk):
        nc = nk // kc
        accs = []
        for j in range(2):
            q = q_ref[:, j * HEAD_PAD:(j + 1) * HEAD_PAD]
            m_run, m_at = None, []
            for c in range(nc):
                st = _dot_nt(k_ref[c * kc:(c + 1) * kc, j * HEAD_PAD:(j + 1) * HEAD_PAD], q)
                mc = jnp.max(st, axis=0, keepdims=True)
                m_run = mc if m_run is None else jnp.maximum(m_run, mc)
                x_refs[j][c * kc:(c + 1) * kc, :] = (st - m_run).astype(BF16)
                m_at.append(m_run)
            acc = None
            for c in range(nc):
                p = jnp.exp2(x_refs[j][c * kc:(c + 1) * kc, :])
                part = _dot(vt_ref[j * HEAD_PAD:(j + 1) * HEAD_PAD, c * kc:(c + 1) * kc], p)
                if c + 1 < nc:
                    part = part * jnp.exp2(m_at[c] - m_run)
                acc = part if acc is None else acc + part
            accs.append(acc)
        outs = [a[0:MLA_V, :] / a[MLA_V:MLA_V + 1, :] for a in accs]
        o_ref[...] = jnp.concatenate(outs, axis=0).T.astype(o_ref.dtype)

    is_ctx = pl.program_id(2) == 0
    pl.when(is_ctx)(lambda: attend(ctx_len))
    pl.when(jnp.logical_not(is_ctx))(lambda: attend(k_ref.shape[0]))


def _attention(q, k, vt):
    bn, t, _ = q.shape
    hp = MLA_HEADS // 2
    return pl.pallas_call(
        functools.partial(_attn_kernel, ctx_len=TM),
        grid=(bn, hp, t // TM),
        in_specs=[pl.BlockSpec((None, TM, 2 * HEAD_PAD), lambda b, h, i: (b, i, h)),
                  pl.BlockSpec((None, t, 2 * HEAD_PAD), lambda b, h, i: (b, 0, h)),
                  pl.BlockSpec((None, 2 * HEAD_PAD, t), lambda b, h, i: (b, h, 0))],
        out_specs=pl.BlockSpec((None, TM, 2 * MLA_V), lambda b, h, i: (b, i, h)),
        out_shape=jax.ShapeDtypeStruct((bn, t, MLA_HEADS * MLA_V), BF16),
        scratch_shapes=[pltpu.VMEM((t, TM), BF16)] * 2,
        compiler_params=_params("parallel", "parallel", "arbitrary"),
        name="attention",
    )(q, k, vt)


def _gla_kernel(qf_ref, vf_ref, df_ref, qb_ref, vb_ref, db_ref, of_ref, ob_ref, sf_ref, sb_ref):
    @pl.when(pl.program_id(1) == 0)
    def _():
        sf_ref[...] = jnp.zeros_like(sf_ref)
        sb_ref[...] = jnp.zeros_like(sb_ref)

    c = GLA_CHUNK
    lane_k = lax.broadcasted_iota(jnp.int32, (c, GLA_K), 1) // GLA_DK
    lane_v = lax.broadcasted_iota(jnp.int32, (c, GLA_V), 1) // GLA_DV
    ai = lax.broadcasted_iota(jnp.int32, (c, GLA_HEADS * c), 0)
    aj = lax.broadcasted_iota(jnp.int32, (c, GLA_HEADS * c), 1) % c
    state_mask = (lax.broadcasted_iota(jnp.int32, (GLA_V, GLA_K), 0) // GLA_DV ==
                  lax.broadcasted_iota(jnp.int32, (GLA_V, GLA_K), 1) // GLA_DK)
    zero = jnp.zeros((), BF16)

    def chunk(g_ref, v_ref, d_ref, o_ref, s_ref, b, r, causal):
        rows = slice(r * c, (r + 1) * c)
        qd = g_ref[b, rows, 0:GLA_K]
        kd = g_ref[b, rows, GLA_K:2 * GLA_K]
        kend = g_ref[b, rows, 2 * GLA_K:3 * GLA_K]
        v = v_ref[b, rows, :]
        dec = d_ref[b, r * c:r * c + 1, :]
        kd_heads = jnp.concatenate([jnp.where(lane_k == h, kd, zero) for h in range(GLA_HEADS)], axis=0)
        v_heads = jnp.concatenate([jnp.where(lane_v == h, v, zero) for h in range(GLA_HEADS)], axis=0)
        att = jnp.where(causal, _dot_nt(qd, kd_heads), 0.0)
        st = s_ref[b]
        o_ref[b, rows, :] = _dot(att.astype(BF16), v_heads) + _dot_nt(qd, st.astype(BF16))
        s_ref[b] = st * dec + jnp.where(state_mask, _dot_tn(v, kend), 0.0)

    n_chunks = qf_ref.shape[1] // c
    for r in range(n_chunks):
        for b in range(qf_ref.shape[0]):
            chunk(qf_ref, vf_ref, df_ref, of_ref, sf_ref, b, r, aj <= ai)
            chunk(qb_ref, vb_ref, db_ref, ob_ref, sb_ref, b, n_chunks - 1 - r, aj >= ai)


def _gla(gl, dec):
    bn, t, _ = gl.shape
    nt = t // TM
    gb = max(g for g in (4, 2, 1) if bn % g == 0)
    fwd = lambda i: i
    bwd = lambda i: jnp.where(i == 0, 0, nt - i)
    spec = lambda w, order, col: pl.BlockSpec((gb, TM, w), lambda b, i: (b, order(i), col))
    return pl.pallas_call(
        _gla_kernel,
        grid=(bn // gb, nt),
        in_specs=[spec(3 * GLA_K, fwd, 0), spec(GLA_V, fwd, 3), spec(GLA_K, fwd, 0),
                  spec(3 * GLA_K, bwd, 1), spec(GLA_V, bwd, 3), spec(GLA_K, bwd, 1)],
        out_specs=[spec(GLA_V, fwd, 0), spec(GLA_V, bwd, 0)],
        out_shape=[jax.ShapeDtypeStruct((bn, t, GLA_V), F32)] * 2,
        scratch_shapes=[pltpu.VMEM((gb, GLA_V, GLA_K), F32)] * 2,
        compiler_params=_params("parallel", "arbitrary"),
        name="gla",
    )(gl, gl, dec, gl, gl, dec)


def _finish_kernel(x_ref, mod_ref, y_ref, yp_ref, yn_ref, cb_ref, sr_ref, sg_ref, att_ref, of_ref, ob_ref,
                   wmla_ref, cw_ref, cbias_ref, wconv_ref, gnorm_ref, wgla_ref, wout_ref, lng_ref, lnb_ref,
                   rwh_ref, rwl_ref, rb_ref, o_ref, u_ref, br_ref, cnt_ref, base_ref, *, alpha):
    i = pl.program_id(1)
    nt = pl.num_programs(1)
    tm = y_ref.shape[0]
    y = y_ref[...]
    has_prev = jnp.logical_and(i != 0, i != 1)
    has_next = jnp.logical_and(i != 0, i != nt - 1)
    prev_row = jnp.where(has_prev, yp_ref[7:8, :], 0.0)
    next_row = jnp.where(has_next, yn_ref[0:1, :], 0.0)
    row = lax.broadcasted_iota(jnp.int32, y.shape, 0)
    y_m1 = jnp.where(row == 0, prev_row, pltpu.roll(y, 1, axis=0))
    y_p1 = jnp.where(row == tm - 1, next_row, pltpu.roll(y, tm - 1, axis=0))
    conv = cbias_ref[...] + y_m1 * cw_ref[0:1, :] + y * cw_ref[1:2, :] + y_p1 * cw_ref[2:3, :]
    conv = cb_ref[...] * conv

    o = of_ref[...] + ob_ref[...]
    gi = lax.broadcasted_iota(jnp.int32, (GLA_V, GLA_V), 0) // GLA_DV
    gj = lax.broadcasted_iota(jnp.int32, (GLA_V, GLA_V), 1) // GLA_DV
    grp = jnp.where(gi == gj, 1.0 / GLA_DV, 0.0).astype(BF16)
    hi, lo = _split_hi_lo(o * o)
    ms = _dot(hi, grp) + _dot(lo, grp)
    gla = o * lax.rsqrt(ms + NORM_EPS) * gnorm_ref[...] * sr_ref[...]

    d = x_ref.shape[1]
    merged = (sg_ref[:, 0:d].astype(F32) * _dot(att_ref[...], wmla_ref[...])
              + sg_ref[:, d:2 * d].astype(F32) * _dot(conv.astype(BF16), wconv_ref[...])
              + sg_ref[:, 2 * d:3 * d].astype(F32) * _dot(gla.astype(BF16), wgla_ref[...]))
    m = _dot(merged.astype(BF16), wout_ref[...])
    x1 = _layer_norm(alpha * x_ref[...] + mod_ref[2:3, :] * m, lng_ref[...], lnb_ref[...])
    o_ref[...] = x1
    _route_tile(x1, mod_ref, rwh_ref, rwl_ref, rb_ref, u_ref, br_ref, cnt_ref, base_ref)


def _finish(x, mods, l, y, cb, sr, sg, att, o_f, o_b, wmla, cw, cbias, wconv, gnorm, wgla, wout, lng, lnb, rw_hi,
            rw_lo, rb, alpha):
    bn, t, d = x.shape
    nt = t // TM
    tok = lambda w: pl.BlockSpec((None, TM, w), lambda b, i: (b, i, 0))
    rows8 = TM // 8
    consts = [wmla, cw, cbias, wconv, gnorm, wgla, wout, lng, lnb, rw_hi, rw_lo, rb]
    return pl.pallas_call(
        functools.partial(_finish_kernel, alpha=alpha),
        grid=(bn, nt),
        in_specs=[tok(d),
                  pl.BlockSpec((None, None, 6, d), lambda b, i: (l, jnp.where(i == 0, bn, b), 0, 0)),
                  tok(CONV_CH),
                  pl.BlockSpec((None, 8, CONV_CH), lambda b, i: (b, jnp.maximum(i * rows8 - 1, 0), 0)),
                  pl.BlockSpec((None, 8, CONV_CH), lambda b, i: (b, jnp.minimum((i + 1) * rows8, nt * rows8 - 1), 0)),
                  tok(CONV_CH), tok(GLA_V), tok(3 * d), tok(MLA_HEADS * MLA_V), tok(GLA_V), tok(GLA_V)]
                 + [_const_spec(a.shape) for a in consts],
        out_specs=[tok(d),
                   pl.BlockSpec((TM * TOKEN_ROWS, LANES), lambda b, i: (b * nt + i, 0)),
                   pl.BlockSpec((None, None, 8, TM), lambda b, i: (b, i, 0, 0)),
                   pl.BlockSpec((BUCKET_ROWS, LANES), lambda b, i: (0, 0))],
        out_shape=[jax.ShapeDtypeStruct((bn, t, d), F32),
                   jax.ShapeDtypeStruct((bn * t * TOKEN_ROWS, LANES), F32),
                   jax.ShapeDtypeStruct((bn, nt, 8, TM), jnp.int32),
                   jax.ShapeDtypeStruct((BUCKET_ROWS, LANES), jnp.int32)],
        scratch_shapes=[pltpu.VMEM((BUCKET_ROWS, LANES), F32)],
        compiler_params=_params("arbitrary", "arbitrary"),
        name="finish",
    )(x, mods, y, y, y, cb, sr, sg, att, o_f, o_b, *consts)


def _route_tile(x1, mod_ref, rwh_ref, rwl_ref, rb_ref, u_ref, br_ref, cnt_ref, base_ref):
    first = jnp.logical_and(pl.program_id(0) == 0, pl.program_id(1) == 0)

    @pl.when(first)
    def _():
        base_ref[...] = jnp.zeros_like(base_ref)

    tm = x1.shape[0]
    u = x1 * (1.0 + mod_ref[4:5, :]) + mod_ref[3:4, :]
    hi, lo = _split_hi_lo(u)
    for j in range(TOKEN_ROWS):
        u_ref[pl.ds(j, tm, stride=TOKEN_ROWS), :] = u[:, j * LANES:(j + 1) * LANES]
    logits = _dot(hi, rwh_ref[...]) + _dot(lo, rwh_ref[...]) + _dot(hi, rwl_ref[...])
    s = _sigmoid(logits.T[0:N_EXPERTS, :])
    sel = s + rb_ref[...]

    sel_e = [sel[e:e + 1, :] for e in range(N_EXPERTS)]
    chosen, gscore = [], []
    for g in range(N_GROUPS):
        members = list(range(g * PER_GROUP, (g + 1) * PER_GROUP))
        picked = []
        for a in members:
            beaten = jnp.zeros((1, tm), F32)
            for b in members:
                if b != a:
                    wins = (sel_e[b] >= sel_e[a]) if b < a else (sel_e[b] > sel_e[a])
                    beaten = beaten + jnp.where(wins, 1.0, 0.0)
            picked.append(beaten < 2.0)
        chosen.append(picked)
        score = jnp.zeros((1, tm), F32)
        for a, p in zip(members, picked):
            score = score + jnp.where(p, sel_e[a], 0.0)
        gscore.append(score)
    best = []
    for g in range(N_GROUPS):
        ok = jnp.ones((1, tm), jnp.bool_)
        for h in range(N_GROUPS):
            if h < g:
                ok = jnp.logical_and(ok, gscore[g] > gscore[h])
            elif h > g:
                ok = jnp.logical_and(ok, gscore[g] >= gscore[h])
        best.append(ok)

    bucket_rows = []
    for g in range(N_GROUPS):
        taken = [jnp.logical_and(best[g], p) for p in chosen[g]]
        for a in range(PER_GROUP):
            for b in range(a + 1, PER_GROUP):
                bucket_rows.append(jnp.where(jnp.logical_and(taken[a], taken[b]), 1.0, 0.0))

    onehot = jnp.concatenate(bucket_rows + [jnp.zeros((BUCKET_ROWS - N_BUCKETS, tm), F32)], axis=0)
    ri = lax.broadcasted_iota(jnp.int32, (tm, tm), 0)
    ci = lax.broadcasted_iota(jnp.int32, (tm, tm), 1)
    before = _dot(onehot.astype(BF16), jnp.where(ri < ci, 1.0, 0.0).astype(BF16))
    base = base_ref[:, 0:1]
    rank = jnp.sum(onehot * (before + base), axis=0, keepdims=True)
    bidx = lax.broadcasted_iota(jnp.int32, onehot.shape, 0).astype(F32)
    bucket = jnp.sum(onehot * bidx, axis=0, keepdims=True)
    br_ref[...] = jnp.concatenate([bucket, rank, jnp.zeros((6, tm), F32)], axis=0).astype(jnp.int32)
    new_base = base + jnp.sum(onehot, axis=1, keepdims=True)
    base_ref[...] = jnp.broadcast_to(new_base, base_ref.shape)
    cnt_ref[...] = jnp.broadcast_to(new_base, cnt_ref.shape).astype(jnp.int32)


def _place_kernel(off_ref, bucket_ref, rank_ref, pos_ref):
    bucket = bucket_ref[...]
    pos = rank_ref[...]
    for b in range(N_BUCKETS):
        pos = pos + jnp.where(bucket == b, off_ref[b], 0)
    pos_ref[...] = pos


def _place(offsets, bucket, rank):
    return pl.pallas_call(
        _place_kernel,
        in_specs=[pl.BlockSpec(memory_space=pltpu.SMEM), pl.BlockSpec(memory_space=pltpu.VMEM),
                  pl.BlockSpec(memory_space=pltpu.VMEM)],
        out_specs=pl.BlockSpec(memory_space=pltpu.VMEM),
        out_shape=jax.ShapeDtypeStruct(bucket.shape, jnp.int32),
        name="place",
    )(offsets, bucket, rank)


def _slot_copy(src_ref, src_slot, dst_ref, dst_slot, sem, rows):
    return pltpu.make_async_copy(src_ref.at[pl.ds(pl.multiple_of(src_slot * rows, rows), rows)],
                                 dst_ref.at[pl.ds(pl.multiple_of(dst_slot * rows, rows), rows)], sem)


def _invert_kernel(pos_ref, off_ref, cnt_ref, used_ref, src_ref):
    def point_at_zero(p, carry):
        src_ref[p] = 0
        return carry

    for b in range(N_BUCKETS):
        start = off_ref[b] + cnt_ref[b]
        lax.fori_loop(start, (start + (TM - 1)) & (-TM), point_at_zero, 0)
    def unused_tile(i, carry):
        for r in range(TM):
            src_ref[i * TM + r] = 0
        return carry

    lax.fori_loop(used_ref[0], src_ref.shape[0] // TM, unused_tile, 0)

    def one(t, carry):
        src_ref[pos_ref[t]] = t
        return carry

    lax.fori_loop(0, pos_ref.shape[0], one, 0, unroll=ISSUE_UNROLL)


def _invert(pos, offsets, counts, n_used, n_sorted_tiles):
    smem = pl.BlockSpec(memory_space=pltpu.SMEM)
    return pl.pallas_call(
        _invert_kernel,
        in_specs=[smem, smem, smem, smem],
        out_specs=smem,
        out_shape=jax.ShapeDtypeStruct((n_sorted_tiles * TM,), jnp.int32),
        name="invert",
    )(pos.reshape(-1), offsets, counts, n_used).reshape(n_sorted_tiles, TM)


def _expert_kernel(e1_ref, e2_ref, used_ref, src_ref, nxt_ref, u_ref, rw1_ref, rw2_ref, wg1_ref, wu1_ref, wd1_ref,
                   wg2_ref, wu2_ref, wd2_ref, y_ref, buf0_ref, buf1_ref, sem):
    del e1_ref, e2_ref
    i = pl.program_id(0)
    n_used = used_ref[0]
    tm = buf0_ref.shape[0] // TOKEN_ROWS
    bufs = (buf0_ref, buf1_ref)

    @pl.when(i == 0)
    def _():
        def issue(r, carry):
            _slot_copy(u_ref, src_ref[0, r], buf0_ref, r, sem.at[0], TOKEN_ROWS).start()
            return carry
        lax.fori_loop(0, tm, issue, 0, unroll=ISSUE_UNROLL)

    def run(s):
        cur, nxt = bufs[s], bufs[1 - s]

        @pl.when(i < n_used)
        def _():
            for r in range(tm):
                _slot_copy(u_ref, nxt_ref[0, r], nxt, r, sem.at[1 - s], TOKEN_ROWS).start()

        pltpu.make_async_copy(u_ref.at[pl.ds(0, tm * TOKEN_ROWS)], cur, sem.at[s]).wait()

        @pl.when(i < n_used)
        def _():
            u = jnp.concatenate([cur[pl.ds(j, tm, stride=TOKEN_ROWS), :] for j in range(TOKEN_ROWS)], axis=1)
            s1 = _sigmoid(jnp.sum(u * rw1_ref[...], axis=-1, keepdims=True))
            s2 = _sigmoid(jnp.sum(u * rw2_ref[...], axis=-1, keepdims=True))
            x = u.astype(BF16)
            h1 = _silu(_dot(x, wg1_ref[...])) * _dot(x, wu1_ref[...])
            h2 = _silu(_dot(x, wg2_ref[...])) * _dot(x, wu2_ref[...])
            y = ((s1 / (s1 + s2)) * _dot(h1.astype(BF16), wd1_ref[...])
                 + (s2 / (s1 + s2)) * _dot(h2.astype(BF16), wd2_ref[...]))
            for j in range(TOKEN_ROWS):
                y_ref[pl.ds(j, tm, stride=TOKEN_ROWS), :] = y[:, j * LANES:(j + 1) * LANES]

    requested = i <= n_used
    pl.when(jnp.logical_and(requested, i % 2 == 0))(lambda: run(0))
    pl.when(jnp.logical_and(requested, i % 2 == 1))(lambda: run(1))

    @pl.when(i >= n_used)
    def _():
        y_ref[...] = jnp.zeros_like(y_ref)


def _experts(tile_e1, tile_e2, n_used, src, u_tok, rw_rows, wg, wu, wd, l):
    n_tiles = src.shape[0]
    d, de = wg.shape[1], wg.shape[2]
    rw = lambda tbl: pl.BlockSpec((None, 1, d), lambda i, e1, e2, n: (tbl(e1, e2)[i], 0, 0))
    up = lambda tbl: pl.BlockSpec((None, d, de), lambda i, e1, e2, n: (l * N_EXPERTS + tbl(e1, e2)[i], 0, 0))
    down = lambda tbl: pl.BlockSpec((None, de, d), lambda i, e1, e2, n: (l * N_EXPERTS + tbl(e1, e2)[i], 0, 0))
    first = lambda e1, e2: e1
    second = lambda e1, e2: e2
    src3 = src.reshape(n_tiles, 1, TM)
    return pl.pallas_call(
        _expert_kernel,
        grid_spec=pltpu.PrefetchScalarGridSpec(
            num_scalar_prefetch=3,
            grid=(n_tiles,),
            in_specs=[pl.BlockSpec((None, 1, TM), lambda i, e1, e2, n: (i, 0, 0), memory_space=pltpu.SMEM),
                      pl.BlockSpec((None, 1, TM), lambda i, e1, e2, n: (jnp.minimum(i + 1, n_tiles - 1), 0, 0),
                                   memory_space=pltpu.SMEM),
                      pl.BlockSpec(memory_space=pl.ANY), rw(first), rw(second),
                      up(first), up(first), down(first), up(second), up(second), down(second)],
            out_specs=pl.BlockSpec((TM * TOKEN_ROWS, LANES), lambda i, e1, e2, n: (i, 0)),
            scratch_shapes=[pltpu.VMEM((TM * TOKEN_ROWS, LANES), F32), pltpu.VMEM((TM * TOKEN_ROWS, LANES), F32),
                            pltpu.SemaphoreType.DMA((2,))],
        ),
        out_shape=jax.ShapeDtypeStruct((n_tiles * TM * TOKEN_ROWS, LANES), F32),
        compiler_params=_params("arbitrary"),
        name="experts",
    )(tile_e1, tile_e2, n_used, src3, src3, u_tok, rw_rows, rw_rows, wg, wu, wd, wg, wu, wd)


def _ffn_out_kernel(pos_ref, nxt_ref, x_ref, mod_ref, ys_ref, lng_ref, lnb_ref, o_ref, buf_ref, sem, *, alpha):
    tm = x_ref.shape[0]
    step = pl.program_id(0) * pl.num_programs(1) + pl.program_id(1)
    n_steps = pl.num_programs(0) * pl.num_programs(1)
    slot = step % 2

    @pl.when(step == 0)
    def _():
        def issue(r, carry):
            _slot_copy(ys_ref, pos_ref[0, r], buf_ref.at[0], r, sem.at[0], TOKEN_ROWS).start()
            return carry
        lax.fori_loop(0, tm, issue, 0, unroll=ISSUE_UNROLL)

    @pl.when(step + 1 < n_steps)
    def _():
        for r in range(tm):
            _slot_copy(ys_ref, nxt_ref[0, r], buf_ref.at[1 - slot], r, sem.at[1 - slot], TOKEN_ROWS).start()
    cur = buf_ref.at[slot]
    pltpu.make_async_copy(ys_ref.at[pl.ds(0, tm * TOKEN_ROWS)], cur, sem.at[slot]).wait()
    f = jnp.concatenate([cur[pl.ds(j, tm, stride=TOKEN_ROWS), :] for j in range(TOKEN_ROWS)], axis=1)
    o_ref[...] = _layer_norm(alpha * x_ref[...] + mod_ref[5:6, :] * f, lng_ref[...], lnb_ref[...])


def _ffn_out(pos, x1, mods, l, ys, lng, lnb, alpha, first_tile):
    bn, t, d = x1.shape
    nt = t // TM
    ng = nt - first_tile
    pos3 = pos.reshape(bn * nt, 1, TM)

    def next_tile(b, i):
        s = jnp.minimum(b * ng + i + 1, bn * ng - 1)
        return ((s // ng) * nt + s % ng + first_tile, 0, 0)

    return pl.pallas_call(
        functools.partial(_ffn_out_kernel, alpha=alpha),
        grid=(bn, ng),
        in_specs=[pl.BlockSpec((None, 1, TM), lambda b, i: (b * nt + i + first_tile, 0, 0), memory_space=pltpu.SMEM),
                  pl.BlockSpec((None, 1, TM), next_tile, memory_space=pltpu.SMEM),
                  pl.BlockSpec((None, TM, d), lambda b, i: (b, i + first_tile, 0)),
                  pl.BlockSpec((None, None, 6, d), lambda b, i: (l, jnp.where(i + first_tile == 0, bn, b), 0, 0)),
                  pl.BlockSpec(memory_space=pl.ANY),
                  _const_spec(lng.shape), _const_spec(lnb.shape)],
        out_specs=pl.BlockSpec((None, TM, d), lambda b, i: (b, i, 0)),
        out_shape=jax.ShapeDtypeStruct((bn, ng * TM, d), F32),
        scratch_shapes=[pltpu.VMEM((2, TM * TOKEN_ROWS, LANES), F32), pltpu.SemaphoreType.DMA((2,))],
        compiler_params=_params("arbitrary", "arbitrary"),
        name="ffn_out",
    )(pos3, pos3, x1, mods, ys, lng, lnb)


def _rope_tables(ctx_len, seq_len):
    t = np.arange(seq_len)
    half = MLA_ROPE // 2
    inv_freq = 1.0 / (ROPE_BASE ** (np.arange(0, half, 2, dtype=np.float32) / half))
    ang_r = (t // GRID_W).astype(np.float32)[:, None] * inv_freq[None, :]
    ang_c = (t % GRID_W).astype(np.float32)[:, None] * inv_freq[None, :]
    ang = np.concatenate([ang_r, ang_r, ang_c, ang_c], axis=-1).astype(np.float32)
    cos = np.concatenate([np.ones((ctx_len, MLA_ROPE), np.float32), np.cos(ang)], axis=0)
    sin = np.concatenate([np.zeros((ctx_len, MLA_ROPE), np.float32), np.sin(ang)], axis=0)
    n = ctx_len + seq_len
    c_tab = np.zeros((n, LANES), np.float32)
    s_tab = np.zeros((n, LANES), np.float32)
    c_tab[:, :MLA_NOPE] = 1.0
    c_tab[:, MLA_NOPE:QK_DIM] = cos
    s_tab[:, MLA_NOPE:QK_DIM] = sin
    scale = np.float32(QK_DIM ** -0.5 * np.log2(np.e))
    return jnp.asarray(np.concatenate([c_tab * scale, s_tab * scale, c_tab, s_tab], axis=1))


def _rotate_cols(w):
    q = MLA_ROPE // 4
    w1, w2, w3, w4 = (w[..., i * q:(i + 1) * q] for i in range(4))
    return jnp.concatenate([-w2, w1, -w4, w3], axis=-1)


def _prep_layer_weights(w_in, mla_wq_b, mla_wkv_b, gla_w_gate, gla_b_gate):
    depth, d, _ = w_in.shape
    splits = np.cumsum([0, MLA_Q_RANK, MLA_KV_RANK, MLA_ROPE, CONV_CH, CONV_CH, CONV_CH, GLA_K, GLA_K, GLA_V,
                        GLA_V, GLA_GATE_RANK, GLA_GATE_RANK, 3 * D_MODEL])
    seg = [w_in[:, :, splits[i]:splits[i + 1]] for i in range(13)]
    zeros = lambda *s: jnp.zeros(s, F32)
    pad_rope = lambda w: jnp.concatenate([zeros(depth, d, MLA_NOPE), w, zeros(depth, d, LANES - QK_DIM)], axis=-1)
    wa = jnp.concatenate(
        [seg[0], seg[1], pad_rope(seg[2]), pad_rope(_rotate_cols(seg[2])), seg[3], seg[4], seg[5], seg[6], seg[7],
         seg[8], seg[9], seg[10], seg[11], zeros(depth, d, LANES - 2 * GLA_GATE_RANK), seg[12]], axis=-1).astype(BF16)

    r = mla_wq_b.shape[1]
    wq = mla_wq_b.reshape(depth, r, MLA_HEADS, QK_DIM)
    q_main = jnp.concatenate([wq, zeros(depth, r, MLA_HEADS, HEAD_PAD - QK_DIM)], axis=-1)
    q_rot = jnp.concatenate([zeros(depth, r, MLA_HEADS, MLA_NOPE), _rotate_cols(wq[..., MLA_NOPE:]),
                             zeros(depth, r, MLA_HEADS, HEAD_PAD - QK_DIM)], axis=-1)
    wq2 = jnp.concatenate([q_main.reshape(depth, r, -1), q_rot.reshape(depth, r, -1)], axis=-1).astype(BF16)

    rk = mla_wkv_b.shape[1]
    wkv = mla_wkv_b.reshape(depth, rk, MLA_HEADS, MLA_NOPE + MLA_V)
    k_part = jnp.concatenate([wkv[..., :MLA_NOPE], zeros(depth, rk, MLA_HEADS, HEAD_PAD - MLA_NOPE)], axis=-1)
    wk = k_part.reshape(depth, rk, -1).astype(BF16)
    v_part = jnp.concatenate([wkv[..., MLA_NOPE:], zeros(depth, rk, MLA_HEADS, HEAD_PAD - MLA_V)], axis=-1)
    wvt = jnp.swapaxes(v_part.reshape(depth, rk, -1), 1, 2).astype(BF16)

    gr = GLA_GATE_RANK
    wgate = jnp.zeros((depth, LANES, 2 * GLA_K), F32)
    wgate = wgate.at[:, 0:gr, 0:GLA_K].set(gla_w_gate[:, 0]).at[:, gr:2 * gr, GLA_K:].set(gla_w_gate[:, 1])
    bgate = gla_b_gate.reshape(depth, 1, 2 * GLA_K)
    return wa, wq2, wk, wvt, wgate.astype(BF16), bgate


def _bucket_tables(counts, n_sorted_tiles):
    padded = ((counts + TM - 1) // TM) * TM
    ends = jnp.cumsum(padded)
    offsets = ends - padded
    n_used = (ends[-1] // TM).astype(jnp.int32)
    tile_start = jnp.arange(n_sorted_tiles, dtype=jnp.int32) * TM
    tile_bucket = jnp.sum((tile_start[:, None] >= ends[None, :]).astype(jnp.int32), axis=1)
    last_bucket = jnp.sum((jnp.maximum(ends[-1] - TM, 0) >= ends).astype(jnp.int32))
    tile_bucket = jnp.where(tile_start < ends[-1], tile_bucket, last_bucket)
    pair_lo = np.array([a for a in range(PER_GROUP) for b in range(a + 1, PER_GROUP)], np.int32)
    pair_hi = np.array([b for a in range(PER_GROUP) for b in range(a + 1, PER_GROUP)], np.int32)
    group = tile_bucket // N_PAIRS
    pair = tile_bucket % N_PAIRS
    e1 = group * PER_GROUP + jnp.asarray(pair_lo)[pair]
    e2 = group * PER_GROUP + jnp.asarray(pair_hi)[pair]
    return offsets.astype(jnp.int32), e1.astype(jnp.int32), e2.astype(jnp.int32), n_used.reshape(1)


def kernel(x, c, ctx, c_ctx, w_mod, b_mod, w_in, mla_q_norm, mla_wq_b, mla_kv_norm, mla_wkv_b, mla_w_o, conv_w,
           conv_b, conv_w_o, gla_w_gate, gla_b_gate, gla_norm, gla_w_o, w_out, ln1_g, ln1_b, router_w, router_b,
           exp_wg, exp_wu, exp_wd, ln2_g, ln2_b):
    bn, seq_len, d = x.shape
    ctx_len = ctx.shape[1]
    depth = w_in.shape[0]
    assert ctx_len == TM and seq_len % TM == 0 and d == D_MODEL and bn < 8
    alpha = (2 * depth) ** 0.25
    t = ctx_len + seq_len
    n_tok = bn * t
    n_sorted_tiles = n_tok // TM + N_BUCKETS

    cond = jnp.concatenate([c, c_ctx[None, :], jnp.zeros((7 - bn, d), F32)], axis=0)
    mods = _modulation(cond, w_mod, b_mod).reshape(depth, 8, 6, d)
    rope_tab = _rope_tables(ctx_len, seq_len)
    wa, wq2, wk, wvt, wgate, bgate = _prep_layer_weights(w_in, mla_wq_b, mla_wkv_b, gla_w_gate, gla_b_gate)
    row = lambda a: a.reshape(depth, 1, a.shape[-1])
    gq, gkv, cbias, gnorm = row(mla_q_norm), row(mla_kv_norm), row(conv_b), row(gla_norm)
    lng1, lnb1, lng2, lnb2 = row(ln1_g), row(ln1_b), row(ln2_g), row(ln2_b)
    wmla, wconv, wgla, wout = (a.astype(BF16) for a in (mla_w_o, conv_w_o, gla_w_o, w_out))
    rw = jnp.concatenate([router_w, jnp.zeros((d, LANES - N_EXPERTS), F32)], axis=1)
    rw_hi = rw.astype(BF16)
    rw_lo = (rw - rw_hi.astype(F32)).astype(BF16)
    rb = router_b.reshape(N_EXPERTS, 1)
    rw_rows = router_w.T.reshape(N_EXPERTS, 1, d)
    de = exp_wg.shape[-1]
    wg = exp_wg.astype(BF16).reshape(depth * N_EXPERTS, d, de)
    wu = exp_wu.astype(BF16).reshape(depth * N_EXPERTS, d, de)
    wd = exp_wd.astype(BF16).reshape(depth * N_EXPERTS, de, d)

    xt = jnp.concatenate([ctx, x], axis=1)
    for l in range(depth):
        q, k, vt, y, cb, gl, dec, sr, sg = _in_proj(xt, mods, l, rope_tab, wa[l], gq[l], wq2[l], gkv[l], wk[l],
                                                    wvt[l], wgate[l], bgate[l])
        att = _attention(q, k, vt)
        o_f, o_b = _gla(gl, dec)
        x1, u_tok, br, counts = _finish(xt, mods, l, y, cb, sr, sg, att, o_f, o_b, wmla[l], conv_w[l], cbias[l],
                                        wconv[l], gnorm[l], wgla[l], wout[l], lng1[l], lnb1[l], rw_hi, rw_lo, rb, alpha)
        counts = counts[:N_BUCKETS, 0]
        offsets, tile_e1, tile_e2, n_used = _bucket_tables(counts, n_sorted_tiles)
        br = br.reshape(bn * (t // TM), 8, TM)
        pos = _place(offsets, br[:, 0, :], br[:, 1, :])
        src = _invert(pos, offsets, counts, n_used, n_sorted_tiles)
        ys = _experts(tile_e1, tile_e2, n_used, src, u_tok, rw_rows, wg, wu, wd, l)
        xt = _ffn_out(pos, x1, mods, l, ys, lng2[l], lnb2[l], alpha, first_tile=int(l == depth - 1))
    return xt
```
